```python
import math
import jax
import jax.numpy as jnp
from jax import lax
import numpy as np

D_MODEL = 4096
BATCH = 4
SEQ = 2048
DEPTH = 2
DEC_BATCH = 8
DEC_SEQ = 1
PAST_LEN = 16384
PAGE_SIZE = 128

DN_HEAD_DIM = 128
DN_HEADS = D_MODEL // (2 * DN_HEAD_DIM)
DN_WIDTH = DN_HEADS * DN_HEAD_DIM
DN_CONV_CH = 3 * DN_WIDTH
CONV_WIDTH = 4
DN_CHUNK = 64
DA_HEAD_DIM = 128
DA_HEADS = D_MODEL // (4 * DA_HEAD_DIM)
DA_WIDTH = DA_HEADS * 2 * DA_HEAD_DIM
Q_BLOCK = 128
ROPE_THETA = 10000.0
D_FF = -(-8 * D_MODEL // (3 * 256)) * 256
RMS_EPS = 1e-6

OFF_DN_Z = DN_CONV_CH
OFF_DN_BETA = OFF_DN_Z + DN_WIDTH
OFF_DN_ALPHA = OFF_DN_BETA + DN_HEADS
OFF_DA_Q = OFF_DN_ALPHA + DN_HEADS
OFF_DA_K = OFF_DA_Q + DA_WIDTH
OFF_DA_V = OFF_DA_K + DA_WIDTH
OFF_GATE_A = OFF_DA_V + DA_WIDTH
OFF_GATE_B = OFF_GATE_A + D_MODEL
IN_COLS = OFF_GATE_B + D_MODEL

kernel_name = 'hybrid_gdn_diffattn_step'


def rms_norm(x, gain):
    xf = x.astype(jnp.float32)
    y = xf * lax.rsqrt(jnp.mean(xf * xf, axis=-1, keepdims=True) + RMS_EPS)
    return (y * gain.astype(jnp.float32)).astype(x.dtype)


def l2_norm(x):
    xf = x.astype(jnp.float32)
    return xf * lax.rsqrt(jnp.sum(xf * xf, axis=-1, keepdims=True) + RMS_EPS)


def rope(x, pos):
    half = x.shape[-1] // 2
    inv_freq = ROPE_THETA ** (-jnp.arange(half, dtype=jnp.float32) / half)
    ang = pos.astype(jnp.float32)[:, None] * inv_freq[None, :]
    cos = jnp.cos(ang)[None, :, None, None, :]
    sin = jnp.sin(ang)[None, :, None, None, :]
    xf = x.astype(jnp.float32)
    x1, x2 = xf[..., :half], xf[..., half:]
    return jnp.concatenate([x1 * cos - x2 * sin, x2 * cos + x1 * sin], axis=-1).astype(x.dtype)


def short_conv(x, buf, w):
    L = x.shape[1]
    xp = jnp.concatenate([buf.astype(x.dtype), x], axis=1)
    y = xp[:, :L] * w[0]
    for i in range(1, CONV_WIDTH):
        y = y + xp[:, i:i + L] * w[i]
    return jax.nn.silu(y), xp[:, L:]


def gated_delta_rule(q, k, v, g, beta, s0):
    B, L, H, _ = q.shape
    n = -(-L // DN_CHUNK)
    pad = n * DN_CHUNK - L

    def chunks(t):
        t = jnp.pad(t, [(0, 0), (0, pad)] + [(0, 0)] * (t.ndim - 2))
        t = t.reshape((B, n, DN_CHUNK) + t.shape[2:])
        return jnp.moveaxis(t, 3, 1)

    q, k, v, g, beta = map(chunks, (q, k, v, g, beta))
    G = jnp.cumsum(g, axis=-1)
    idx = jnp.arange(DN_CHUNK)
    incl = idx[:, None] >= idx[None, :]
    strict = idx[:, None] > idx[None, :]
    decay = jnp.exp(jnp.where(incl, G[..., :, None] - G[..., None, :], -jnp.inf))
    kk = jnp.einsum('bhnid,bhnjd->bhnij', k, k)
    m = jnp.where(strict, kk * decay * beta[..., :, None], 0.0)
    eye = jnp.eye(DN_CHUNK, dtype=jnp.float32)
    t_inv = lax.linalg.triangular_solve(eye + m, jnp.broadcast_to(eye, m.shape),
                                        left_side=True, lower=True, unit_diagonal=True)
    u_base = t_inv @ (v * beta[..., None])
    w = t_inv @ (k * (beta * jnp.exp(G))[..., None])
    qk = jnp.einsum('bhnid,bhnjd->bhnij', q, k) * decay
    q_dec = q * jnp.exp(G)[..., None]
    k_dec = k * jnp.exp(G[..., -1:] - G)[..., None]
    g_tot = jnp.exp(G[..., -1])

    def step(s, xs):
        u_b, w_c, qk_c, qd_c, kd_c, gt_c = xs
        u = u_b - w_c @ s
        o = qd_c @ s + qk_c @ u
        s = s * gt_c[..., None, None] + jnp.einsum('bhcd,bhce->bhde', kd_c, u)
        return s, o

    xs = [jnp.moveaxis(t, 2, 0) for t in (u_base, w, qk, q_dec, k_dec, g_tot)]
    s_final, o = lax.scan(step, s0, xs)
    o = jnp.transpose(o, (1, 0, 3, 2, 4)).reshape(B, n * DN_CHUNK, H, -1)
    return o[:, :L], s_final


def diff_attn_block(q, k, v, q_pos, k_pos, lam):
    s = jnp.einsum('bqhsd,bkhsd->bhsqk', q, k, preferred_element_type=jnp.float32) * (DA_HEAD_DIM ** -0.5)
    mask = k_pos[None, :] <= q_pos[:, None]
    s = jnp.where(mask[None, None, None], s, -jnp.inf)
    p = jax.nn.softmax(s, axis=-1)
    a = p[:, :, 0] - lam * p[:, :, 1]
    return jnp.einsum('bhqk,bkhe->bqhe', a, v.astype(jnp.float32))


def diff_attention(q, k, v, q_pos, k_pos, lam):
    B, Lq = q.shape[0], q.shape[1]
    if Lq <= Q_BLOCK:
        return diff_attn_block(q, k, v, q_pos, k_pos, lam)
    n = -(-Lq // Q_BLOCK)
    pad = n * Q_BLOCK - Lq
    qp = jnp.pad(q, ((0, 0), (0, pad), (0, 0), (0, 0), (0, 0)))
    pp = jnp.concatenate([q_pos, jnp.full((pad,), q_pos[-1], q_pos.dtype)])
    qb = jnp.moveaxis(qp.reshape(B, n, Q_BLOCK, *q.shape[2:]), 1, 0)
    pb = pp.reshape(n, Q_BLOCK)
    ob = lax.map(lambda a: diff_attn_block(a[0], k, v, a[1], k_pos, lam), (qb, pb))
    o = jnp.moveaxis(ob, 0, 1).reshape(B, n * Q_BLOCK, *ob.shape[3:])
    return o[:, :Lq]


def trunk_layer(x, conv_buf, s0, k_past, v_past, layer_idx, p):
    B, L, _ = x.shape
    f32 = jnp.float32
    past_len = k_past.shape[1]
    pos = past_len + jnp.arange(L)
    k_pos = jnp.arange(past_len + L)
    h = rms_norm(x, p['attn_norm'])
    proj = h @ p['w_in']

    qkv, conv_new = short_conv(proj[..., :DN_CONV_CH], conv_buf, p['dn_conv'])
    qkv = qkv.reshape(B, L, 3, DN_HEADS, DN_HEAD_DIM)
    dq = l2_norm(qkv[:, :, 0]) * (DN_HEAD_DIM ** -0.5)
    dk = l2_norm(qkv[:, :, 1])
    dv = qkv[:, :, 2].astype(f32)
    z = proj[..., OFF_DN_Z:OFF_DN_BETA].reshape(B, L, DN_HEADS, DN_HEAD_DIM).astype(f32)
    beta = jax.nn.sigmoid(proj[..., OFF_DN_BETA:OFF_DN_ALPHA].astype(f32))
    g = -jnp.exp(p['dn_a_log'].astype(f32)) * jax.nn.softplus(
        proj[..., OFF_DN_ALPHA:OFF_DA_Q].astype(f32) + p['dn_dt_bias'].astype(f32))
    o_dn, s_new = gated_delta_rule(dq, dk, dv, g, beta, s0.astype(f32))
    o_dn = (rms_norm(o_dn, p['dn_out_norm']) * jax.nn.silu(z)).reshape(B, L, DN_WIDTH).astype(x.dtype)

    qa = proj[..., OFF_DA_Q:OFF_DA_K].reshape(B, L, DA_HEADS, 2, DA_HEAD_DIM)
    ka = proj[..., OFF_DA_K:OFF_DA_V].reshape(B, L, DA_HEADS, 2, DA_HEAD_DIM)
    va = proj[..., OFF_DA_V:OFF_GATE_A].reshape(B, L, DA_HEADS, 2 * DA_HEAD_DIM)
    qa = rope(rms_norm(qa, p['da_q_norm']), pos)
    ka = rope(rms_norm(ka, p['da_k_norm']), pos)
    k_rows = ka.reshape(B, L, DA_HEADS, 2 * DA_HEAD_DIM)
    keys = jnp.concatenate([k_past.astype(x.dtype), k_rows], axis=1).reshape(
        B, past_len + L, DA_HEADS, 2, DA_HEAD_DIM)
    vals = jnp.concatenate([v_past.astype(x.dtype), va], axis=1)
    lam_init = 0.8 - 0.6 * math.exp(-0.3 * layer_idx)
    lp = p['da_lambda'].astype(f32)
    lam = jnp.exp(jnp.sum(lp[0] * lp[1])) - jnp.exp(jnp.sum(lp[2] * lp[3])) + lam_init
    o_da = diff_attention(qa, keys, vals, pos, k_pos, lam)
    o_da = (rms_norm(o_da, p['da_sub_norm']) * (1.0 - lam_init)).reshape(B, L, DA_WIDTH).astype(x.dtype)

    gate_a = jax.nn.sigmoid(proj[..., OFF_GATE_A:OFF_GATE_B])
    gate_b = jax.nn.sigmoid(proj[..., OFF_GATE_B:])
    merged = gate_a * (o_dn @ p['w_branch_a']) + gate_b * (o_da @ p['w_branch_b'])
    x = x + merged @ p['w_out']

    gu = rms_norm(x, p['ffn_norm']) @ p['w_gate_up']
    x = x + (jax.nn.silu(gu[..., :D_FF]) * gu[..., D_FF:]) @ p['w_down']
    return x, k_rows, va, s_new.astype(x.dtype), conv_new


def setup_inputs(seed: int = 0) -> dict:
    key = jax.random.key(seed)
    ks = jax.random.split(key, 24)
    f32 = jnp.float32
    n_pages = PAST_LEN // PAGE_SIZE
    n_pool = (DEC_BATCH * n_pages * 5) // 4

    def nrm(k, shape, scale):
        return jax.random.normal(k, shape, f32) * scale

    def gain(k, shape):
        return 1.0 + 0.02 * jax.random.normal(k, shape, f32)

    dt = jnp.exp(jax.random.uniform(ks[11], (DEPTH, DN_HEADS), f32,
                                    minval=math.log(1e-3), maxval=math.log(1e-1)))
    perm = jax.random.permutation(ks[6], n_pool)[:DEC_BATCH * n_pages]
    kv_shape = (DEPTH, n_pool, PAGE_SIZE, DA_HEADS, 2 * DA_HEAD_DIM)
    return {
        'x_prompt': nrm(ks[0], (BATCH, SEQ, D_MODEL), 1.0),
        'x_sample': nrm(ks[1], (DEC_BATCH, DEC_SEQ, D_MODEL), 1.0),
        'cache_k': nrm(ks[2], kv_shape, 1.0),
        'cache_v': nrm(ks[3], kv_shape, 1.0),
        'state_delta': nrm(ks[4], (DEPTH, DEC_BATCH, DN_HEADS, DN_HEAD_DIM, DN_HEAD_DIM), DN_HEAD_DIM ** -0.5),
        'state_conv': nrm(ks[5], (DEPTH, DEC_BATCH, CONV_WIDTH - 1, DN_CONV_CH), 1.0),
        'page_table': perm.reshape(DEC_BATCH, n_pages).astype(jnp.int32),
        'attn_norm': gain(ks[7], (DEPTH, D_MODEL)),
        'w_in': nrm(ks[8], (DEPTH, D_MODEL, IN_COLS), D_MODEL ** -0.5),
        'dn_conv': nrm(ks[9], (DEPTH, CONV_WIDTH, DN_CONV_CH), CONV_WIDTH ** -0.5),
        'dn_a_log': jnp.log(jax.random.uniform(ks[10], (DEPTH, DN_HEADS), f32, minval=1.0, maxval=16.0)),
        'dn_dt_bias': dt + jnp.log(-jnp.expm1(-dt)),
        'dn_out_norm': gain(ks[12], (DEPTH, DN_HEAD_DIM)),
        'da_q_norm': gain(ks[13], (DEPTH, DA_HEAD_DIM)),
        'da_k_norm': gain(ks[14], (DEPTH, DA_HEAD_DIM)),
        'da_lambda': nrm(ks[15], (DEPTH, 4, DA_HEAD_DIM), 0.1),
        'da_sub_norm': gain(ks[16], (DEPTH, 2 * DA_HEAD_DIM)),
        'w_branch_a': nrm(ks[17], (DEPTH, DN_WIDTH, D_MODEL), DN_WIDTH ** -0.5),
        'w_branch_b': nrm(ks[18], (DEPTH, DA_WIDTH, D_MODEL), DA_WIDTH ** -0.5),
        'w_out': nrm(ks[19], (DEPTH, D_MODEL, D_MODEL), D_MODEL ** -0.5),
        'ffn_norm': gain(ks[20], (DEPTH, D_MODEL)),
        'w_gate_up': nrm(ks[21], (DEPTH, D_MODEL, 2 * D_FF), D_MODEL ** -0.5),
        'w_down': nrm(ks[22], (DEPTH, D_FF, D_MODEL), D_FF ** -0.5),
    }


def reference(x_prompt, x_sample, cache_k, cache_v, state_delta, state_conv, page_table,
              attn_norm, w_in, dn_conv, dn_a_log, dn_dt_bias, dn_out_norm, da_q_norm, da_k_norm,
              da_lambda, da_sub_norm, w_branch_a, w_branch_b, w_out, ffn_norm, w_gate_up, w_down):
    b, db = x_prompt.shape[0], x_sample.shape[0]
    n_pages = page_table.shape[1]
    y_prompt, y_sample = x_prompt, x_sample
    kp, vp, ksm, vsm, sp, ssm, cp, csm = [], [], [], [], [], [], [], []
    for l in range(DEPTH):
        p = {'attn_norm': attn_norm[l], 'w_in': w_in[l], 'dn_conv': dn_conv[l],
             'dn_a_log': dn_a_log[l], 'dn_dt_bias': dn_dt_bias[l], 'dn_out_norm': dn_out_norm[l],
             'da_q_norm': da_q_norm[l], 'da_k_norm': da_k_norm[l], 'da_lambda': da_lambda[l],
             'da_sub_norm': da_sub_norm[l], 'w_branch_a': w_branch_a[l], 'w_branch_b': w_branch_b[l],
             'w_out': w_out[l], 'ffn_norm': ffn_norm[l], 'w_gate_up': w_gate_up[l], 'w_down': w_down[l]}
        empty = jnp.zeros((b, 0, DA_HEADS, 2 * DA_HEAD_DIM), x_prompt.dtype)
        conv0 = jnp.zeros((b, CONV_WIDTH - 1, DN_CONV_CH), x_prompt.dtype)
        s0 = jnp.zeros((b, DN_HEADS, DN_HEAD_DIM, DN_HEAD_DIM), jnp.float32)
        y_prompt, k_new, v_new, s_new, c_new = trunk_layer(y_prompt, conv0, s0, empty, empty, l, p)
        kp.append(k_new); vp.append(v_new); sp.append(s_new); cp.append(c_new)
        past_shape = (db, n_pages * PAGE_SIZE, DA_HEADS, 2 * DA_HEAD_DIM)
        k_past = cache_k[l, page_table].reshape(past_shape)
        v_past = cache_v[l, page_table].reshape(past_shape)
        y_sample, k_new, v_new, s_new, c_new = trunk_layer(
            y_sample, state_conv[l], state_delta[l], k_past, v_past, l, p)
        ksm.append(k_new); vsm.append(v_new); ssm.append(s_new); csm.append(c_new)
    return (y_prompt, y_sample, jnp.stack(kp), jnp.stack(vp), jnp.stack(ksm), jnp.stack(vsm),
            jnp.stack(sp), jnp.stack(ssm), jnp.stack(cp), jnp.stack(csm))
```

```python
import functools
import math

import jax
import jax.numpy as jnp
from jax import lax
from jax.experimental import pallas as pl
from jax.experimental.pallas import tpu as pltpu

F32 = jnp.float32
BF16 = jnp.bfloat16

RMS_EPS = 1e-6
ROPE_THETA = 10000.0
CONV_WIDTH = 4
DN_HEAD_DIM = 128
DN_CHUNK = 64
DA_HEAD_DIM = 128
LANES = 128
SUBLANES = 8
TAIL = 16
CONV_PAD = 8
VMEM_LIMIT = 56 * 1024 * 1024
HIGHEST = lax.Precision.HIGHEST


def _pick_tile(n, cap, unit=LANES):
    best = None
    t = unit
    while t <= min(n, cap):
        if n % t == 0:
            best = t
        t += unit
    assert best is not None, (n, cap)
    return best


def _nt_dot(a, b, precision=None):
    return lax.dot_general(a, b, (((1,), (1,)), ((), ())),
                           preferred_element_type=F32, precision=precision)


def _dot(a, b, precision=None):
    return jnp.dot(a, b, preferred_element_type=F32, precision=precision)


def _silu(x):
    return x * jax.nn.sigmoid(x)


def _params(sem):
    return pltpu.CompilerParams(dimension_semantics=sem, vmem_limit_bytes=VMEM_LIMIT)


def _row_cases(i, n_full, fn):
    @pl.when(i < n_full)
    def _():
        fn(slice(None))

    @pl.when(i == n_full)
    def _():
        fn(slice(0, TAIL))


def _rmsnorm_kernel(x_ref, g_ref, o_ref, *, n_full):
    def body(rows):
        x = x_ref[rows, :]
        ms = jnp.mean(x * x, axis=-1, keepdims=True)
        o_ref[rows, :] = (x * lax.rsqrt(ms + RMS_EPS) * g_ref[...]).astype(o_ref.dtype)

    _row_cases(pl.program_id(0), n_full, body)


def _rmsnorm(x, gain, m_full, tr=256):
    mp, d = x.shape
    n_full = m_full // tr
    return pl.pallas_call(
        functools.partial(_rmsnorm_kernel, n_full=n_full),
        grid=(n_full + 1,),
        in_specs=[pl.BlockSpec((tr, d), lambda i: (i, 0)),
                  pl.BlockSpec((1, d), lambda i: (0, 0))],
        out_specs=pl.BlockSpec((tr, d), lambda i: (i, 0)),
        out_shape=jax.ShapeDtypeStruct((mp, d), BF16),
        compiler_params=_params(("arbitrary",)),
        name="rmsnorm",
    )(x, gain.reshape(1, d))


CAST_ROWS = 256


def _load_weight(i, b_ref, bscr):
    if bscr is None:
        return b_ref

    @pl.when(i == 0)
    def _():
        k = b_ref.shape[0]
        step = CAST_ROWS if k % CAST_ROWS == 0 else k

        def body(c, carry):
            r0 = pl.multiple_of(c * step, step)
            bscr[pl.ds(r0, step), :] = b_ref[pl.ds(r0, step), :].astype(BF16)
            return carry

        lax.fori_loop(0, k // step, body, 0)

    return bscr


def _mm_plain_kernel(a_ref, b_ref, o_ref, *scr, n_full):
    i = pl.program_id(1)
    w = _load_weight(i, b_ref, scr[0] if scr else None)

    def body(rows):
        o_ref[rows, :] = _dot(a_ref[rows, :], w[...]).astype(o_ref.dtype)

    _row_cases(i, n_full, body)


def _mm_resid_kernel(a_ref, b_ref, r_ref, o_ref, *scr, n_full):
    i = pl.program_id(1)
    w = _load_weight(i, b_ref, scr[0] if scr else None)

    def body(rows):
        o_ref[rows, :] = r_ref[rows, :] + _dot(a_ref[rows, :], w[...])

    _row_cases(i, n_full, body)


def _mm_swiglu_kernel(a_ref, bg_ref, bu_ref, o_ref, sg, su, *, n_full):
    i = pl.program_id(1)
    wg = _load_weight(i, bg_ref, sg)
    wu = _load_weight(i, bu_ref, su)

    def body(rows):
        a = a_ref[rows, :]
        g = _dot(a, wg[...])
        u = _dot(a, wu[...])
        o_ref[rows, :] = (_silu(g) * u).astype(o_ref.dtype)

    _row_cases(i, n_full, body)


def _mm_merge_kernel(a1_ref, a2_ref, b1_ref, b2_ref, g1_ref, g2_ref, o_ref, s1, s2, *, n_full):
    i = pl.program_id(1)
    w1 = _load_weight(i, b1_ref, s1)
    w2 = _load_weight(i, b2_ref, s2)

    def body(rows):
        y1 = _dot(a1_ref[rows, :], w1[...])
        y2 = _dot(a2_ref[rows, :], w2[...])
        o_ref[rows, :] = (jax.nn.sigmoid(g1_ref[rows, :]) * y1
                          + jax.nn.sigmoid(g2_ref[rows, :]) * y2).astype(o_ref.dtype)

    _row_cases(i, n_full, body)


def _row_tile(m_full, cap):
    return _pick_tile(m_full, cap, unit=TAIL)


def _matmul(a, b, *, m_full, n_cols, tn, tm, out_dtype, b_col0=0, resid=None, name="mm"):
    mp, k = a.shape
    n_full = m_full // tm
    assert b_col0 % tn == 0 and n_cols % tn == 0
    joff = b_col0 // tn
    scratch = [] if b.dtype == BF16 else [pltpu.VMEM((k, tn), BF16)]
    in_specs = [pl.BlockSpec((tm, k), lambda j, i: (i, 0)),
                pl.BlockSpec((k, tn), lambda j, i: (0, j + joff))]
    args = [a, b]
    if resid is None:
        kern = functools.partial(_mm_plain_kernel, n_full=n_full)
    else:
        kern = functools.partial(_mm_resid_kernel, n_full=n_full)
        in_specs.append(pl.BlockSpec((tm, tn), lambda j, i: (i, j)))
        args.append(resid)
    return pl.pallas_call(
        kern,
        grid=(n_cols // tn, n_full + 1),
        in_specs=in_specs,
        out_specs=pl.BlockSpec((tm, tn), lambda j, i: (i, j)),
        out_shape=jax.ShapeDtypeStruct((mp, n_cols), out_dtype),
        scratch_shapes=scratch,
        compiler_params=_params(("arbitrary", "arbitrary")),
        name=name,
    )(*args)


def _matmul_swiglu(a, w_gate_up, *, m_full, d_ff, tn, tm):
    mp, k = a.shape
    n_full = m_full // tm
    uoff = d_ff // tn
    return pl.pallas_call(
        functools.partial(_mm_swiglu_kernel, n_full=n_full),
        grid=(d_ff // tn, n_full + 1),
        in_specs=[pl.BlockSpec((tm, k), lambda j, i: (i, 0)),
                  pl.BlockSpec((k, tn), lambda j, i: (0, j)),
                  pl.BlockSpec((k, tn), lambda j, i: (0, j + uoff))],
        out_specs=pl.BlockSpec((tm, tn), lambda j, i: (i, j)),
        out_shape=jax.ShapeDtypeStruct((mp, d_ff), BF16),
        scratch_shapes=[pltpu.VMEM((k, tn), BF16), pltpu.VMEM((k, tn), BF16)],
        compiler_params=_params(("arbitrary", "arbitrary")),
        name="mm_swiglu",
    )(a, w_gate_up, w_gate_up)


def _matmul_merge(a1, a2, b1, b2, gates, *, m_full, gate_col0, tn, tm):
    mp, k = a1.shape
    n = b1.shape[1]
    n_full = m_full // tm
    assert gate_col0 % tn == 0
    g1off = gate_col0 // tn
    g2off = g1off + n // tn
    return pl.pallas_call(
        functools.partial(_mm_merge_kernel, n_full=n_full),
        grid=(n // tn, n_full + 1),
        in_specs=[pl.BlockSpec((tm, k), lambda j, i: (i, 0)),
                  pl.BlockSpec((tm, k), lambda j, i: (i, 0)),
                  pl.BlockSpec((k, tn), lambda j, i: (0, j)),
                  pl.BlockSpec((k, tn), lambda j, i: (0, j)),
                  pl.BlockSpec((tm, tn), lambda j, i: (i, j + g1off)),
                  pl.BlockSpec((tm, tn), lambda j, i: (i, j + g2off))],
        out_specs=pl.BlockSpec((tm, tn), lambda j, i: (i, j)),
        out_shape=jax.ShapeDtypeStruct((mp, n), BF16),
        scratch_shapes=[pltpu.VMEM((k, tn), BF16), pltpu.VMEM((k, tn), BF16)],
        compiler_params=_params(("arbitrary", "arbitrary")),
        name="mm_merge",
    )(a1, a2, b1, b2, gates, gates)


def _delta_kernel(alog_ref, dtb_ref,
                  qp_ref, kp_ref, vp_ref, z_ref, ba_ref,
                  cbq_ref, cbk_ref, cbv_ref, cwq_ref, cwk_ref, cwv_ref,
                  s0_ref, gn_ref,
                  o_ref, sn_ref,
                  xq_s, xk_s, xv_s, ub_s, wq_s, qk_s, kdt_s, gt_s,
                  *, seq, valid, n_heads):
    C = DN_CHUNK
    D = DN_HEAD_DIM
    h = pl.program_id(1)
    n_chunks = seq // C

    for xs, cb, xp in ((xq_s, cbq_ref, qp_ref), (xk_s, cbk_ref, kp_ref), (xv_s, cbv_ref, vp_ref)):
        xs[0:CONV_PAD, :] = cb[...]
        xs[CONV_PAD:, :] = xp[...]

    row = lax.broadcasted_iota(jnp.int32, (C, C), 0)
    col = lax.broadcasted_iota(jnp.int32, (C, C), 1)
    incl = row >= col
    strict = row > col
    eye = (row == col).astype(F32)
    tril = incl.astype(F32)
    ones = jnp.ones((C, C), F32)
    lane = lax.broadcasted_iota(jnp.int32, (C, LANES), 1)
    rowd = lax.broadcasted_iota(jnp.int32, (D, D), 0)
    cold = lax.broadcasted_iota(jnp.int32, (D, D), 1)
    eye_d = (rowd == cold).astype(BF16)

    neg_a = -jnp.exp(jnp.full((1, 1), alog_ref[h], F32))
    dt_bias = jnp.full((1, 1), dtb_ref[h], F32)

    def conv_silu(xs, cw_ref, r0):
        xc = xs[pl.ds(r0, C + CONV_PAD), :]
        w = cw_ref[...]
        first = CONV_PAD - (CONV_WIDTH - 1)
        y = xc[first:first + C] * w[0:1]
        for t in range(1, CONV_WIDTH):
            y = y + xc[first + t:first + t + C] * w[t:t + 1]
        return _silu(y)

    def l2n(x):
        return x * lax.rsqrt(jnp.sum(x * x, axis=-1, keepdims=True) + RMS_EPS)

    def prep(c, carry):
        r0 = pl.multiple_of(c * C, C)
        q = l2n(conv_silu(xq_s, cwq_ref, r0)) * (D ** -0.5)
        k = l2n(conv_silu(xk_s, cwk_ref, r0))
        v = conv_silu(xv_s, cwv_ref, r0)
        ba = ba_ref[pl.ds(r0, C), :]
        bcol = jnp.sum(jnp.where(lane == h, ba, 0.0), axis=-1, keepdims=True)
        acol = jnp.sum(jnp.where(lane == n_heads + h, ba, 0.0), axis=-1, keepdims=True)
        beta = jax.nn.sigmoid(bcol)
        x = acol + dt_bias
        g = neg_a * (jnp.maximum(x, 0.0) + jnp.log1p(jnp.exp(-jnp.abs(x))))
        if valid < seq:
            live = (r0 + lax.broadcasted_iota(jnp.int32, (C, 1), 0)) < valid
            k = jnp.where(live, k, 0.0)
            v = jnp.where(live, v, 0.0)
            beta = jnp.where(live, beta, 0.0)
            g = jnp.where(live, g, 0.0)

        gsum = _dot(tril, jnp.broadcast_to(g, (C, LANES)), HIGHEST)
        gi = gsum[:, :C]
        gj = _dot(ones, jnp.where(row == col, gi, 0.0), HIGHEST)
        decay = jnp.exp(jnp.where(incl, gi - gj, -jnp.inf))
        kbf = k.astype(BF16)
        kk = _nt_dot(kbf, kbf)
        m = jnp.where(strict, kk * decay * beta, 0.0)
        x_pow = -m
        t_inv = eye + x_pow
        for _ in range(int(math.log2(C)) - 1):
            x_pow = _dot(x_pow, x_pow, HIGHEST)
            t_inv = t_inv + _dot(t_inv, x_pow, HIGHEST)
        e_g = jnp.exp(gsum)
        rhs = jnp.concatenate([v * beta, k * (beta * e_g)], axis=1).astype(BF16)
        tu = _dot(t_inv.astype(BF16), rhs)
        g_last = gsum[C - 1:C, :]
        k_dec = (k * jnp.exp(g_last - gsum)).astype(BF16)
        r2 = pl.multiple_of(c * 2 * C, 2 * C)
        ub_s[pl.ds(r0, C), :] = tu[:, :D]
        wq_s[pl.ds(r2, C), :] = tu[:, D:].astype(BF16)
        wq_s[pl.ds(r2 + C, C), :] = (q * e_g).astype(BF16)
        qk_s[pl.ds(r0, C), :] = (_nt_dot(q.astype(BF16), kbf) * decay).astype(BF16)
        kdt_s[pl.ds(pl.multiple_of(c * D, D), D), :] = _nt_dot(eye_d, k_dec).astype(BF16)
        gt_s[pl.ds(pl.multiple_of(c * SUBLANES, SUBLANES), SUBLANES), :] = jnp.broadcast_to(
            jnp.exp(g_last), (SUBLANES, LANES))
        return carry

    lax.fori_loop(0, n_chunks, prep, 0)

    gain = gn_ref[...]

    def scan(c, s):
        r0 = pl.multiple_of(c * C, C)
        sb = s.astype(BF16)
        ws = _dot(wq_s[pl.ds(pl.multiple_of(c * 2 * C, 2 * C), 2 * C), :], sb)
        u = (ub_s[pl.ds(r0, C), :] - ws[:C]).astype(BF16)
        o = ws[C:] + _dot(qk_s[pl.ds(r0, C), :], u)
        g_tot = gt_s[pl.ds(pl.multiple_of(c * SUBLANES, SUBLANES), 1), :]
        s = s * g_tot + _dot(kdt_s[pl.ds(pl.multiple_of(c * D, D), D), :], u)
        ms = jnp.mean(o * o, axis=-1, keepdims=True)
        y = o * lax.rsqrt(ms + RMS_EPS) * gain
        o_ref[pl.ds(r0, C), :] = (y * _silu(z_ref[pl.ds(r0, C), :])).astype(o_ref.dtype)
        return s

    sn_ref[...] = lax.fori_loop(0, n_chunks, scan, s0_ref[...])


def _delta_branch(proj1, conv_buf8, conv_w, s0, a_log, dt_bias, out_gain, *, batch, seq, valid, n_heads):
    D = DN_HEAD_DIM
    C = DN_CHUNK
    H = n_heads
    n_chunks = seq // C
    blk = lambda off: pl.BlockSpec((seq, D), lambda b, h: (b, off + h))
    cbs = lambda off: pl.BlockSpec((None, CONV_PAD, D), lambda b, h: (b, 0, off + h))
    cws = lambda off: pl.BlockSpec((CONV_WIDTH, D), lambda b, h: (0, off + h))
    smem = pl.BlockSpec(memory_space=pltpu.SMEM)
    o, s_new = pl.pallas_call(
        functools.partial(_delta_kernel, seq=seq, valid=valid, n_heads=H),
        grid=(batch, H),
        in_specs=[smem, smem,
                  blk(0), blk(H), blk(2 * H), blk(3 * H),
                  pl.BlockSpec((seq, LANES), lambda b, h: (b, 4 * H)),
                  cbs(0), cbs(H), cbs(2 * H), cws(0), cws(H), cws(2 * H),
                  pl.BlockSpec((None, None, D, D), lambda b, h: (b, h, 0, 0)),
                  pl.BlockSpec((1, D), lambda b, h: (0, 0))],
        out_specs=[pl.BlockSpec((seq, D), lambda b, h: (b, h)),
                   pl.BlockSpec((None, None, D, D), lambda b, h: (b, h, 0, 0))],
        out_shape=[jax.ShapeDtypeStruct((batch * seq, H * D), BF16),
                   jax.ShapeDtypeStruct((batch, H, D, D), F32)],
        scratch_shapes=[pltpu.VMEM((seq + CONV_PAD, D), F32)] * 3 + [
            pltpu.VMEM((seq, D), F32),
            pltpu.VMEM((2 * seq, D), BF16),
            pltpu.VMEM((seq, C), BF16),
            pltpu.VMEM((n_chunks * D, C), BF16),
            pltpu.VMEM((n_chunks * SUBLANES, LANES), F32)],
        compiler_params=_params(("arbitrary", "arbitrary")),
        name="delta",
    )(a_log, dt_bias, proj1, proj1, proj1, proj1, proj1,
      conv_buf8, conv_buf8, conv_buf8, conv_w, conv_w, conv_w, s0, out_gain.reshape(1, D))
    return o, s_new


def _norm_rope(x, gain, cos, sin):
    ms = jnp.mean(x * x, axis=-1, keepdims=True)
    y = x * lax.rsqrt(ms + RMS_EPS) * gain
    return y * cos + pltpu.roll(y, DA_HEAD_DIM // 2, 1) * sin


def _lambda(lp, lam_init):
    return (jnp.exp(jnp.sum(lp[0:1] * lp[1:2], axis=1, keepdims=True))
            - jnp.exp(jnp.sum(lp[2:3] * lp[3:4], axis=1, keepdims=True)) + lam_init)


def _da_kernel(q_ref, k_ref, v_ref, cos_ref, sin_ref, qg_ref, kg_ref, sg_ref, lp_ref,
               o_ref, krow_ref, kbf_s, vbf_s, *, seq, tq, lam_init):
    dh = DA_HEAD_DIM
    qi = pl.program_id(2)
    scale = dh ** -0.5

    @pl.when(qi == 0)
    def _():
        kg = kg_ref[...]

        def body(c, carry):
            r0 = pl.multiple_of(c * tq, tq)
            kb = k_ref[pl.ds(r0, tq), :]
            cos = cos_ref[pl.ds(r0, tq), :]
            sin = sin_ref[pl.ds(r0, tq), :]
            kr = jnp.concatenate([_norm_rope(kb[:, :dh], kg, cos, sin),
                                  _norm_rope(kb[:, dh:], kg, cos, sin)], axis=1)
            krow_ref[pl.ds(r0, tq), :] = kr
            kbf_s[pl.ds(r0, tq), :] = kr.astype(BF16)
            vbf_s[pl.ds(r0, tq), :] = v_ref[pl.ds(r0, tq), :].astype(BF16)
            return carry

        lax.fori_loop(0, seq // tq, body, 0)

    q0 = pl.multiple_of(qi * tq, tq)
    qb = q_ref[...]
    cos = cos_ref[pl.ds(q0, tq), :]
    sin = sin_ref[pl.ds(q0, tq), :]
    qg = qg_ref[...]
    q1 = _norm_rope(qb[:, :dh], qg, cos, sin).astype(BF16)
    q2 = _norm_rope(qb[:, dh:], qg, cos, sin).astype(BF16)
    lam = _lambda(lp_ref[...], lam_init)
    causal = (lax.broadcasted_iota(jnp.int32, (tq, tq), 1)
              <= lax.broadcasted_iota(jnp.int32, (tq, tq), 0))

    def scores(kt, masked):
        kb = kbf_s[pl.ds(pl.multiple_of(kt * tq, tq), tq), :]
        s1 = _nt_dot(q1, kb[:, :dh]) * scale
        s2 = _nt_dot(q2, kb[:, dh:]) * scale
        if masked:
            s1 = jnp.where(causal, s1, -jnp.inf)
            s2 = jnp.where(causal, s2, -jnp.inf)
        return s1, s2

    def stat_update(s, m, l):
        m_new = jnp.maximum(m, jnp.max(s, axis=-1, keepdims=True))
        l_new = l * jnp.exp(m - m_new) + jnp.sum(jnp.exp(s - m_new), axis=-1, keepdims=True)
        return m_new, l_new

    def stats(kt, carry, masked=False):
        m1, l1, m2, l2 = carry
        s1, s2 = scores(kt, masked)
        m1, l1 = stat_update(s1, m1, l1)
        m2, l2 = stat_update(s2, m2, l2)
        return m1, l1, m2, l2

    neg = jnp.full((tq, 1), -jnp.inf, F32)
    zero = jnp.zeros((tq, 1), F32)
    carry = lax.fori_loop(0, qi, stats, (neg, zero, neg, zero))
    m1, l1, m2, l2 = stats(qi, carry, masked=True)
    inv1 = 1.0 / l1
    inv2 = 1.0 / l2

    def weighted(kt, acc, masked=False):
        s1, s2 = scores(kt, masked)
        a = jnp.exp(s1 - m1) * inv1 - lam * (jnp.exp(s2 - m2) * inv2)
        vb = vbf_s[pl.ds(pl.multiple_of(kt * tq, tq), tq), :]
        return acc + _dot(a.astype(BF16), vb)

    acc = lax.fori_loop(0, qi, weighted, jnp.zeros((tq, 2 * dh), F32))
    o = weighted(qi, acc, masked=True)
    ms = jnp.mean(o * o, axis=-1, keepdims=True)
    o_ref[...] = (o * lax.rsqrt(ms + RMS_EPS) * sg_ref[...] * (1.0 - lam_init)).astype(o_ref.dtype)


def _diff_attn_prompt(proj3, cos, sin, q_gain, k_gain, sub_gain, lam_p, *, batch, seq, n_heads, lam_init, tq):
    dh = DA_HEAD_DIM
    H = n_heads
    nq = seq // tq
    full = lambda shape: pl.BlockSpec(shape, lambda b, h, qi: (0, 0))
    o, k_rows = pl.pallas_call(
        functools.partial(_da_kernel, seq=seq, tq=tq, lam_init=lam_init),
        grid=(batch, H, nq),
        in_specs=[pl.BlockSpec((tq, 2 * dh), lambda b, h, qi: (b * nq + qi, h)),
                  pl.BlockSpec((seq, 2 * dh), lambda b, h, qi: (b, H + h)),
                  pl.BlockSpec((seq, 2 * dh), lambda b, h, qi: (b, 2 * H + h)),
                  full((seq, dh)), full((seq, dh)), full((1, dh)), full((1, dh)),
                  full((1, 2 * dh)), full((4, dh))],
        out_specs=[pl.BlockSpec((tq, 2 * dh), lambda b, h, qi: (b * nq + qi, h)),
                   pl.BlockSpec((seq, 2 * dh), lambda b, h, qi: (b, h))],
        out_shape=[jax.ShapeDtypeStruct((batch * seq, H * 2 * dh), BF16),
                   jax.ShapeDtypeStruct((batch * seq, H * 2 * dh), F32)],
        scratch_shapes=[pltpu.VMEM((seq, 2 * dh), BF16), pltpu.VMEM((seq, 2 * dh), BF16)],
        compiler_params=_params(("arbitrary", "arbitrary", "arbitrary")),
        name="diff_attn_prompt",
    )(proj3, proj3, proj3, cos, sin, q_gain.reshape(1, dh), k_gain.reshape(1, dh),
      sub_gain.reshape(1, 2 * dh), lam_p)
    return o, k_rows


def _decode_kernel(pt_ref, q_ref, kn_ref, vn_ref, cos_ref, sin_ref, qg_ref, kg_ref, sg_ref, lp_ref,
                   kc_ref, vc_ref,
                   o_ref, kout_ref,
                   qmat_s, sc_s, a_s, acc_s, new_s,
                   *, n_pages, n_heads, lam_init):
    dh = DA_HEAD_DIM
    H = n_heads
    R = 2 * H
    W = H * 2 * dh
    p = pl.program_id(1)
    scale = dh ** -0.5
    rowi = lax.broadcasted_iota(jnp.int32, (R, W), 0)
    coli = lax.broadcasted_iota(jnp.int32, (R, W), 1)

    @pl.when(p == 0)
    def _():
        cos = cos_ref[...]
        sin = sin_ref[...]
        qn = _norm_rope(q_ref[...], qg_ref[...], cos, sin)
        kn = _norm_rope(kn_ref[...], kg_ref[...], cos, sin)
        kout_ref[...] = kn
        tgt = (rowi % H) * 2 + rowi // H
        qmat_s[...] = jnp.where(coli // dh == tgt, jnp.concatenate([qn] * (W // dh), axis=1),
                                0.0).astype(BF16)
        s_new = jnp.sum(qn.astype(BF16).astype(F32) * kn.astype(BF16).astype(F32),
                        axis=-1, keepdims=True) * scale
        new_s[...] = jnp.broadcast_to(s_new, (R, LANES))
        acc_s[...] = jnp.zeros_like(acc_s)

    @pl.when(p < n_pages)
    def _():
        kp = kc_ref[...].astype(BF16)
        sc_s[p] = _nt_dot(qmat_s[...], kp) * scale

    @pl.when(p == n_pages - 1)
    def _():
        s = sc_s[...]
        s_new = new_s[...]
        m = jnp.maximum(jnp.max(jnp.max(s, axis=0), axis=-1, keepdims=True), s_new[:, 0:1])
        e = jnp.exp(s - m)
        e_new = jnp.exp(s_new - m)
        l = jnp.sum(jnp.sum(e, axis=0), axis=-1, keepdims=True) + e_new[:, 0:1]
        inv = 1.0 / l
        lam = _lambda(lp_ref[...], lam_init)
        pr = e * inv
        a = pr[:, :H, :] - lam * pr[:, H:, :]
        a_s[...] = jnp.concatenate([a, jnp.zeros_like(a)], axis=1).astype(BF16)
        pn = e_new * inv
        new_s[...] = jnp.concatenate([pn[:H] - lam * pn[H:], jnp.zeros((R - H, LANES), F32)], axis=0)

    @pl.when(p >= n_pages)
    def _():
        vp = vc_ref[...].astype(BF16)
        acc_s[...] += _dot(a_s[p - n_pages], vp)

    @pl.when(p == 2 * n_pages - 1)
    def _():
        acc = jnp.where(coli // (2 * dh) == rowi, acc_s[...], 0.0)
        o = acc[:, 0:2 * dh]
        for c in range(1, H):
            o = o + acc[:, c * 2 * dh:(c + 1) * 2 * dh]
        a_new = new_s[...][:H, 0:1].astype(BF16).astype(F32)
        o = o[:H] + a_new * vn_ref[...].astype(BF16).astype(F32)
        ms = jnp.mean(o * o, axis=-1, keepdims=True)
        o_ref[...] = (o * lax.rsqrt(ms + RMS_EPS) * sg_ref[...] * (1.0 - lam_init)).astype(o_ref.dtype)


def _diff_attn_sample(page_table, q_sh, k_sh, v_new, cos, sin, q_gain, k_gain, sub_gain, lam_p,
                      cache_k, cache_v, *, layer, lam_init):
    dh = DA_HEAD_DIM
    B, n_pages = page_table.shape
    H = v_new.shape[1]
    R = 2 * H
    page, W = cache_k.shape[2], cache_k.shape[3]
    per_b = lambda shape: pl.BlockSpec((None,) + shape, lambda b, p, pt: (b, 0, 0))
    full = lambda shape: pl.BlockSpec(shape, lambda b, p, pt: (0, 0))
    grid_spec = pltpu.PrefetchScalarGridSpec(
        num_scalar_prefetch=1,
        grid=(B, 2 * n_pages),
        in_specs=[per_b((R, dh)), per_b((R, dh)), per_b((H, 2 * dh)),
                  full((1, dh)), full((1, dh)), full((1, dh)), full((1, dh)),
                  full((1, 2 * dh)), full((4, dh)),
                  pl.BlockSpec((None, None, page, W),
                               lambda b, p, pt: (layer, pt[b, jnp.minimum(p, n_pages - 1)], 0, 0)),
                  pl.BlockSpec((None, None, page, W),
                               lambda b, p, pt: (layer, pt[b, jnp.maximum(p - n_pages, 0)], 0, 0))],
        out_specs=[per_b((H, 2 * dh)), per_b((R, dh))],
        scratch_shapes=[pltpu.VMEM((R, W), BF16),
                        pltpu.VMEM((n_pages, R, page), F32),
                        pltpu.VMEM((n_pages, R, page), BF16),
                        pltpu.VMEM((R, W), F32),
                        pltpu.VMEM((R, LANES), F32)])
    o, k_out = pl.pallas_call(
        functools.partial(_decode_kernel, n_pages=n_pages, n_heads=H, lam_init=lam_init),
        grid_spec=grid_spec,
        out_shape=[jax.ShapeDtypeStruct((B, H, 2 * dh), BF16),
                   jax.ShapeDtypeStruct((B, R, dh), F32)],
        compiler_params=_params(("arbitrary", "arbitrary")),
        name="diff_attn_sample",
    )(page_table, q_sh, k_sh, v_new, cos, sin, q_gain.reshape(1, dh), k_gain.reshape(1, dh),
      sub_gain.reshape(1, 2 * dh), lam_p, cache_k, cache_v)
    return o, k_out


def _rope_tables(pos):
    half = DA_HEAD_DIM // 2
    inv_freq = ROPE_THETA ** (-jnp.arange(half, dtype=F32) / half)
    ang = pos.astype(F32)[:, None] * inv_freq[None, :]
    cos = jnp.cos(ang)
    sin = jnp.sin(ang)
    return jnp.concatenate([cos, cos], axis=1), jnp.concatenate([-sin, sin], axis=1)


def kernel(x_prompt, x_sample, cache_k, cache_v, state_delta, state_conv, page_table, attn_norm, w_in, dn_conv, dn_a_log, dn_dt_bias, dn_out_norm, da_q_norm, da_k_norm, da_lambda, da_sub_norm, w_branch_a, w_branch_b, w_out, ffn_norm, w_gate_up, w_down):
    B, L, D = x_prompt.shape
    DB = x_sample.shape[0]
    depth = w_in.shape[0]
    assert x_sample.shape[1] == 1 and DB <= TAIL and L % DN_CHUNK == 0
    n_pages = page_table.shape[1]
    page = cache_k.shape[2]
    past_len = n_pages * page
    dn_w = dn_conv.shape[2] // 3
    dn_h = dn_w // DN_HEAD_DIM
    da_h = cache_k.shape[3]
    da_w = da_h * 2 * DA_HEAD_DIM
    d_ff = w_down.shape[1]
    off_beta = 4 * dn_w
    off_da_q = off_beta + 2 * dn_h
    n1 = off_beta + LANES
    n3 = 3 * da_w + 2 * D
    assert 2 * dn_h <= LANES and w_in.shape[2] == off_da_q + n3

    m_full = B * L
    tm = _row_tile(m_full, 1024)
    x = jnp.concatenate([x_prompt.reshape(m_full, D), x_sample.reshape(DB, D),
                         jnp.zeros((TAIL - DB, D), F32)], axis=0)
    cache_k2 = cache_k.reshape(cache_k.shape[0], cache_k.shape[1], page, da_w)
    cache_v2 = cache_v.reshape(cache_v.shape[0], cache_v.shape[1], page, da_w)
    cos_p, sin_p = _rope_tables(jnp.arange(L))
    cos_s, sin_s = _rope_tables(past_len + jnp.arange(1))
    tq = _pick_tile(L, 256)
    zero_buf = jnp.zeros((B, CONV_PAD, 3 * dn_w), F32)
    zero_state = jnp.zeros((B, dn_h, DN_HEAD_DIM, DN_HEAD_DIM), F32)

    kp, vp, ksm, vsm, sp, ssm, cp, csm = [], [], [], [], [], [], [], []
    for l in range(depth):
        lam_init = 0.8 - 0.6 * math.exp(-0.3 * l)
        h = _rmsnorm(x, attn_norm[l], m_full)
        proj1 = _matmul(h, w_in[l], m_full=m_full, n_cols=n1, tn=_pick_tile(n1, 640), tm=tm,
                        out_dtype=F32, name="mm_proj_dn")
        w3 = w_in[l][:, off_da_q:].astype(BF16)
        proj3 = _matmul(h, w3, m_full=m_full, n_cols=n3, tn=_pick_tile(n3, 512), tm=tm,
                        out_dtype=F32, name="mm_proj_da")

        o_dn_p, s_p = _delta_branch(proj1, zero_buf, dn_conv[l], zero_state, dn_a_log[l], dn_dt_bias[l],
                                    dn_out_norm[l], batch=B, seq=L, valid=L, n_heads=dn_h)
        tail1 = proj1[m_full:m_full + DB]
        seq_s = DN_CHUNK
        proj1_s = jnp.zeros((DB, seq_s, n1), F32).at[:, 0].set(tail1).reshape(DB * seq_s, n1)
        buf_s = jnp.concatenate([jnp.zeros((DB, CONV_PAD - (CONV_WIDTH - 1), 3 * dn_w), F32),
                                 state_conv[l]], axis=1)
        o_dn_s, s_s = _delta_branch(proj1_s, buf_s, dn_conv[l], state_delta[l], dn_a_log[l], dn_dt_bias[l],
                                    dn_out_norm[l], batch=DB, seq=seq_s, valid=1, n_heads=dn_h)
        o_dn = jnp.concatenate([o_dn_p, o_dn_s.reshape(DB, seq_s, dn_w)[:, 0],
                                jnp.zeros((TAIL - DB, dn_w), BF16)], axis=0)
        sp.append(s_p)
        ssm.append(s_s)
        cp.append(proj1[:m_full, :3 * dn_w].reshape(B, L, 3 * dn_w)[:, L - (CONV_WIDTH - 1):])
        csm.append(jnp.concatenate([state_conv[l][:, 1:], tail1[:, None, :3 * dn_w]], axis=1))

        o_da_p, k_rows_p = _diff_attn_prompt(proj3, cos_p, sin_p, da_q_norm[l], da_k_norm[l],
                                             da_sub_norm[l], da_lambda[l], batch=B, seq=L,
                                             n_heads=da_h, lam_init=lam_init, tq=tq)
        tail3 = proj3[m_full:m_full + DB]
        to_sh = lambda t: t.reshape(DB, da_h, 2, DA_HEAD_DIM).transpose(0, 2, 1, 3).reshape(
            DB, 2 * da_h, DA_HEAD_DIM)
        v_new = tail3[:, 2 * da_w:3 * da_w].reshape(DB, da_h, 2 * DA_HEAD_DIM)
        o_da_s, k_new = _diff_attn_sample(page_table, to_sh(tail3[:, :da_w]), to_sh(tail3[:, da_w:2 * da_w]),
                                          v_new, cos_s, sin_s, da_q_norm[l], da_k_norm[l], da_sub_norm[l],
                                          da_lambda[l], cache_k2, cache_v2, layer=l, lam_init=lam_init)
        o_da = jnp.concatenate([o_da_p, o_da_s.reshape(DB, da_w), jnp.zeros((TAIL - DB, da_w), BF16)], axis=0)
        kp.append(k_rows_p.reshape(B, L, da_h, 2 * DA_HEAD_DIM))
        vp.append(proj3[:m_full, 2 * da_w:3 * da_w].reshape(B, L, da_h, 2 * DA_HEAD_DIM))
        ksm.append(k_new.reshape(DB, 2, da_h, DA_HEAD_DIM).transpose(0, 2, 1, 3).reshape(
            DB, 1, da_h, 2 * DA_HEAD_DIM))
        vsm.append(v_new.reshape(DB, 1, da_h, 2 * DA_HEAD_DIM))

        merged = _matmul_merge(o_dn, o_da, w_branch_a[l], w_branch_b[l], proj3, m_full=m_full,
                               gate_col0=3 * da_w, tn=_pick_tile(D, 512), tm=tm)
        x = _matmul(merged, w_out[l], m_full=m_full, n_cols=D, tn=_pick_tile(D, 512), tm=tm,
                    out_dtype=F32, resid=x, name="mm_out")
        hn = _rmsnorm(x, ffn_norm[l], m_full)
        act = _matmul_swiglu(hn, w_gate_up[l], m_full=m_full, d_ff=d_ff, tn=_pick_tile(d_ff, 256), tm=tm)
        x = _matmul(act, w_down[l].astype(BF16), m_full=m_full, n_cols=D, tn=_pick_tile(D, 512),
                    tm=_row_tile(m_full, 512), out_dtype=F32, resid=x, name="mm_down")

    y_prompt = x[:m_full].reshape(B, L, D)
    y_sample = x[m_full:m_full + DB].reshape(DB, 1, D)
    return (y_prompt, y_sample, jnp.stack(kp), jnp.stack(vp), jnp.stack(ksm), jnp.stack(vsm),
            jnp.stack(sp), jnp.stack(ssm), jnp.stack(cp), jnp.stack(csm))
```

```python
import functools
import math

import jax
import jax.numpy as jnp
from jax import lax
from jax.experimental import pallas as pl
from jax.experimental.pallas import tpu as pltpu

F32 = jnp.float32
BF16 = jnp.bfloat16

RMS_EPS = 1e-6
ROPE_THETA = 10000.0
CONV_WIDTH = 4
DN_HEAD_DIM = 128
DN_CHUNK = 64
DA_HEAD_DIM = 128
LANES = 128
SUBLANES = 8
TAIL = 16
CONV_PAD = 8
VMEM_LIMIT = 56 * 1024 * 1024
DN_HEADS_PER_STEP = 2
PAGES_PER_STEP = 4


def _pick_tile(n, cap, unit=LANES):
    best = None
    t = unit
    while t <= min(n, cap):
        if n % t == 0:
            best = t
        t += unit
    assert best is not None, (n, cap)
    return best


def _nt_dot(a, b):
    return lax.dot_general(a, b, (((1,), (1,)), ((), ())), preferred_element_type=F32)


def _dot(a, b):
    return jnp.dot(a, b, preferred_element_type=F32)


def _silu(x):
    return x * jax.nn.sigmoid(x)


def _split3(x):
    hi = x.astype(BF16)
    r = x - hi.astype(F32)
    mid = r.astype(BF16)
    return hi, mid, (r - mid.astype(F32)).astype(BF16)


def _params(sem):
    return pltpu.CompilerParams(dimension_semantics=sem, vmem_limit_bytes=VMEM_LIMIT)


def _rmsnorm_kernel(x_ref, xs_ref, g_ref, o_ref, os_ref):
    def norm(x):
        ms = jnp.mean(x * x, axis=-1, keepdims=True)
        return (x * lax.rsqrt(ms + RMS_EPS) * g_ref[...]).astype(BF16)

    o_ref[...] = norm(x_ref[...])

    @pl.when(pl.program_id(0) == 0)
    def _():
        os_ref[...] = norm(xs_ref[...])


def _rmsnorm(x, xs, gain, tr=256):
    m, d = x.shape
    return pl.pallas_call(
        _rmsnorm_kernel,
        grid=(m // tr,),
        in_specs=[pl.BlockSpec((tr, d), lambda i: (i, 0)),
                  pl.BlockSpec((TAIL, d), lambda i: (0, 0)),
                  pl.BlockSpec((1, d), lambda i: (0, 0))],
        out_specs=[pl.BlockSpec((tr, d), lambda i: (i, 0)),
                   pl.BlockSpec((TAIL, d), lambda i: (0, 0))],
        out_shape=[jax.ShapeDtypeStruct((m, d), BF16), jax.ShapeDtypeStruct((TAIL, d), BF16)],
        compiler_params=_params(("arbitrary",)),
        name="rmsnorm",
    )(x, xs, gain.reshape(1, d))


CAST_ROWS = 256


def _cast_weight(b_ref, bscr, b_next=None, shift=0):
    k, tn = bscr.shape
    step = CAST_ROWS if k % CAST_ROWS == 0 else k

    def body(c, carry):
        r0 = pl.multiple_of(c * step, step)
        w = b_ref[pl.ds(r0, step), :]
        if b_next is not None:
            w = jnp.concatenate([w, b_next[pl.ds(r0, step), :]], axis=1)[:, shift:shift + tn]
        bscr[pl.ds(r0, step), :] = w.astype(BF16)
        return carry

    lax.fori_loop(0, k // step, body, 0)


def _mm_plain_kernel(*refs, shift, cast, resid):
    refs = list(refs)
    a_ref, as_ref, b_ref = refs[:3]
    del refs[:3]
    bn_ref = refs.pop(0) if shift else None
    r_ref, rs_ref = (refs.pop(0), refs.pop(0)) if resid else (None, None)
    o_ref, os_ref = refs[:2]
    w = refs[2] if cast else b_ref
    first = pl.program_id(1) == 0

    if cast:
        @pl.when(first)
        def _():
            _cast_weight(b_ref, w, bn_ref, shift)

    def out(a, r):
        y = _dot(a[...], w[...])
        return y if r is None else r[...] + y

    o_ref[...] = out(a_ref, r_ref).astype(o_ref.dtype)

    @pl.when(first)
    def _():
        os_ref[...] = out(as_ref, rs_ref).astype(os_ref.dtype)


def _matmul(a, a_s, b, *, n_cols, tn, tm, out_dtype, b_col0=0, shift=0, resid=None, name="mm"):
    m, k = a.shape
    assert m % tm == 0 and b_col0 % tn == 0 and n_cols % tn == 0 and 0 <= shift < LANES
    joff = b_col0 // tn
    cast = b.dtype != BF16
    assert cast or not shift
    in_specs = [pl.BlockSpec((tm, k), lambda j, i: (i, 0)),
                pl.BlockSpec((TAIL, k), lambda j, i: (0, 0)),
                pl.BlockSpec((k, tn), lambda j, i: (0, j + joff))]
    args = [a, a_s, b]
    if shift:
        per = tn // LANES
        in_specs.append(pl.BlockSpec((k, LANES), lambda j, i: (0, (j + joff + 1) * per)))
        args.append(b)
    if resid is not None:
        in_specs += [pl.BlockSpec((tm, tn), lambda j, i: (i, j)),
                     pl.BlockSpec((TAIL, tn), lambda j, i: (0, j))]
        args += list(resid)
    return pl.pallas_call(
        functools.partial(_mm_plain_kernel, shift=shift, cast=cast, resid=resid is not None),
        grid=(n_cols // tn, m // tm),
        in_specs=in_specs,
        out_specs=[pl.BlockSpec((tm, tn), lambda j, i: (i, j)),
                   pl.BlockSpec((TAIL, tn), lambda j, i: (0, j))],
        out_shape=[jax.ShapeDtypeStruct((m, n_cols), out_dtype),
                   jax.ShapeDtypeStruct((TAIL, n_cols), out_dtype)],
        scratch_shapes=[pltpu.VMEM((k, tn), BF16)] if cast else [],
        compiler_params=_params(("arbitrary", "arbitrary")),
        name=name,
    )(*args)


def _mm_swiglu_kernel(a_ref, as_ref, bg_ref, bu_ref, o_ref, os_ref, sg, su):
    first = pl.program_id(1) == 0

    @pl.when(first)
    def _():
        _cast_weight(bg_ref, sg)
        _cast_weight(bu_ref, su)

    def out(a):
        return (_silu(_dot(a, sg[...])) * _dot(a, su[...])).astype(BF16)

    o_ref[...] = out(a_ref[...])

    @pl.when(first)
    def _():
        os_ref[...] = out(as_ref[...])


def _matmul_swiglu(a, a_s, w_gate_up, *, d_ff, tn, tm):
    m, k = a.shape
    uoff = d_ff // tn
    return pl.pallas_call(
        _mm_swiglu_kernel,
        grid=(d_ff // tn, m // tm),
        in_specs=[pl.BlockSpec((tm, k), lambda j, i: (i, 0)),
                  pl.BlockSpec((TAIL, k), lambda j, i: (0, 0)),
                  pl.BlockSpec((k, tn), lambda j, i: (0, j)),
                  pl.BlockSpec((k, tn), lambda j, i: (0, j + uoff))],
        out_specs=[pl.BlockSpec((tm, tn), lambda j, i: (i, j)),
                   pl.BlockSpec((TAIL, tn), lambda j, i: (0, j))],
        out_shape=[jax.ShapeDtypeStruct((m, d_ff), BF16), jax.ShapeDtypeStruct((TAIL, d_ff), BF16)],
        scratch_shapes=[pltpu.VMEM((k, tn), BF16), pltpu.VMEM((k, tn), BF16)],
        compiler_params=_params(("arbitrary", "arbitrary")),
        name="mm_swiglu",
    )(a, a_s, w_gate_up, w_gate_up)


def _mm_merge_kernel(a1_ref, a2_ref, a1s_ref, a2s_ref, b1_ref, b2_ref, g1_ref, g2_ref, g1s_ref, g2s_ref,
                     o_ref, os_ref, s1, s2):
    first = pl.program_id(1) == 0

    @pl.when(first)
    def _():
        _cast_weight(b1_ref, s1)
        _cast_weight(b2_ref, s2)

    def out(a1, a2, g1, g2):
        return (jax.nn.sigmoid(g1[...]) * _dot(a1[...], s1[...])
                + jax.nn.sigmoid(g2[...]) * _dot(a2[...], s2[...])).astype(BF16)

    o_ref[...] = out(a1_ref, a2_ref, g1_ref, g2_ref)

    @pl.when(first)
    def _():
        os_ref[...] = out(a1s_ref, a2s_ref, g1s_ref, g2s_ref)


def _matmul_merge(a1, a2, a1s, a2s, b1, b2, gates, gates_s, *, gate_col0, tn, tm):
    m, k = a1.shape
    n = b1.shape[1]
    assert gate_col0 % tn == 0
    g1off = gate_col0 // tn
    g2off = g1off + n // tn
    row = lambda blk: pl.BlockSpec((tm, blk), lambda j, i: (i, 0))
    tail = lambda blk: pl.BlockSpec((TAIL, blk), lambda j, i: (0, 0))
    wt = pl.BlockSpec((k, tn), lambda j, i: (0, j))
    return pl.pallas_call(
        _mm_merge_kernel,
        grid=(n // tn, m // tm),
        in_specs=[row(k), row(k), tail(k), tail(k), wt, wt,
                  pl.BlockSpec((tm, tn), lambda j, i: (i, j + g1off)),
                  pl.BlockSpec((tm, tn), lambda j, i: (i, j + g2off)),
                  pl.BlockSpec((TAIL, tn), lambda j, i: (0, j + g1off)),
                  pl.BlockSpec((TAIL, tn), lambda j, i: (0, j + g2off))],
        out_specs=[pl.BlockSpec((tm, tn), lambda j, i: (i, j)),
                   pl.BlockSpec((TAIL, tn), lambda j, i: (0, j))],
        out_shape=[jax.ShapeDtypeStruct((m, n), BF16), jax.ShapeDtypeStruct((TAIL, n), BF16)],
        scratch_shapes=[pltpu.VMEM((k, tn), BF16), pltpu.VMEM((k, tn), BF16)],
        compiler_params=_params(("arbitrary", "arbitrary")),
        name="mm_merge",
    )(a1, a2, a1s, a2s, b1, b2, gates, gates, gates_s, gates_s)


DN_SUPER = 4


def _delta_kernel(alog_ref, dtb_ref,
                  qp_ref, kp_ref, vp_ref, z_ref, ba_ref,
                  cbq_ref, cbk_ref, cbv_ref, cwq_ref, cwk_ref, cwv_ref,
                  s0_ref, gn_ref,
                  o_ref, sn_ref,
                  xq_s, xk_s, xv_s, aq_s, n_s, oc_s, gt_s,
                  *, seq, valid, n_heads):
    C = DN_CHUNK
    D = DN_HEAD_DIM
    P = DN_HEADS_PER_STEP
    hp = pl.program_id(1)
    n_chunks = seq // C
    U = min(DN_SUPER, n_chunks)
    R = U * C
    AQ = D + C

    for xs, cb, xp in ((xq_s, cbq_ref, qp_ref), (xk_s, cbk_ref, kp_ref), (xv_s, cbv_ref, vp_ref)):
        xs[0:CONV_PAD, :] = cb[...]
        xs[CONV_PAD:, :] = xp[...]

    row = lax.broadcasted_iota(jnp.int32, (R, R), 0)
    col = lax.broadcasted_iota(jnp.int32, (R, R), 1)
    same = (row // C) == (col // C)
    incl = same & (row >= col)
    strict = same & (row > col)
    diag = row == col
    eye = diag.astype(F32)
    tril_bf = incl.astype(BF16)
    ones_bf = jnp.ones((R, R), BF16)
    last_bf = (col == (row // C) * C + (C - 1)).astype(BF16)
    lane = lax.broadcasted_iota(jnp.int32, (R, LANES), 1)

    def conv_silu(xs, cw_ref, r0):
        xc = xs[pl.ds(r0, R + CONV_PAD), :]
        w = cw_ref[...]
        first = CONV_PAD - (CONV_WIDTH - 1)
        y = xc[first:first + R] * w[0:1]
        for t in range(1, CONV_WIDTH):
            y = y + xc[first + t:first + t + R] * w[t:t + 1]
        return _silu(y)

    def l2n(x):
        return x * lax.rsqrt(jnp.sum(x * x, axis=-1, keepdims=True) + RMS_EPS)

    def exact_dot(sel_bf, x):
        return sum(_dot(sel_bf, t) for t in _split3(x))

    def each(fn, *lists):
        return [fn(*args) for args in zip(*lists)]

    triu_same = same & (row <= col)

    def prep(sc, carry):
        r0 = pl.multiple_of(sc * R, R)
        heads = list(range(P))
        qc = conv_silu(xq_s, cwq_ref, r0)
        kc = conv_silu(xk_s, cwk_ref, r0)
        vc = conv_silu(xv_s, cwv_ref, r0)
        ba = ba_ref[pl.ds(r0, R), :]
        q = [l2n(qc[:, p * D:(p + 1) * D]) * (D ** -0.5) for p in heads]
        k = [l2n(kc[:, p * D:(p + 1) * D]) for p in heads]
        v = [vc[:, p * D:(p + 1) * D] for p in heads]
        beta, g = [], []
        for p in heads:
            h = hp * P + p
            neg_a = -jnp.exp(jnp.full((1, 1), alog_ref[h], F32))
            dt_bias = jnp.full((1, 1), dtb_ref[h], F32)
            bcol = jnp.sum(jnp.where(lane == h, ba, 0.0), axis=-1, keepdims=True)
            acol = jnp.sum(jnp.where(lane == n_heads + h, ba, 0.0), axis=-1, keepdims=True)
            x = acol + dt_bias
            beta.append(jax.nn.sigmoid(bcol))
            g.append(neg_a * (jnp.maximum(x, 0.0) + jnp.log1p(jnp.exp(-jnp.abs(x)))))
        if valid < seq:
            live = (r0 + lax.broadcasted_iota(jnp.int32, (R, 1), 0)) < valid
            dead = lambda t: jnp.where(live, t, 0.0)
            k, v, beta, g = each(dead, k), each(dead, v), each(dead, beta), each(dead, g)

        g_wide = each(lambda t: jnp.broadcast_to(t, (R, max(R, D))), g)
        gsum = each(lambda t: exact_dot(tril_bf, t[:, :D]), g_wide)
        gj = each(lambda t: exact_dot(ones_bf, jnp.where(triu_same, t[:, :R], 0.0)), g_wide)
        gi = each(lambda t: jnp.concatenate([t] * (R // D), axis=1) if R > D else t[:, :R], gsum)
        decay = each(lambda a, b: jnp.exp(jnp.where(incl, a - b, -jnp.inf)), gi, gj)
        kbf = each(lambda t: t.astype(BF16), k)
        m = each(lambda kb, d, b: jnp.where(strict, _nt_dot(kb, kb) * d * b, 0.0), kbf, decay, beta)
        x_pow = each(lambda t: -t, m)
        t_inv = each(lambda t: eye + t, x_pow)
        for _ in range(int(math.log2(C)) - 1):
            xb = each(lambda t: t.astype(BF16), x_pow)
            x_pow = each(lambda t: _dot(t, t), xb)
            t_inv = each(lambda t, xp: t + _dot(t.astype(BF16), xp.astype(BF16)), t_inv, x_pow)
        m_hi = each(lambda t: t.astype(BF16), m)
        m_lo = each(lambda t, hi: (t - hi.astype(F32)).astype(BF16), m, m_hi)
        t_hi = each(lambda t: t.astype(BF16), t_inv)
        t_lo = each(lambda t, hi: (t - hi.astype(F32)).astype(BF16), t_inv, t_hi)
        resid = each(lambda t, mh, ml, th, tl: eye - t - (_dot(mh, th) + (_dot(mh, tl) + _dot(ml, th))),
                     t_inv, m_hi, m_lo, t_hi, t_lo)
        t_inv = each(lambda t, th, r: t + _dot(th, r.astype(BF16)), t_inv, t_hi, resid)

        e_g = each(jnp.exp, gsum)
        wu = each(lambda t, kk, vv, b, e: _dot(t.astype(BF16), jnp.concatenate(
            [kk * (b * e), vv * b], axis=1).astype(BF16)).astype(BF16), t_inv, k, v, beta, e_g)
        qk = each(lambda qq, kb, d: (_nt_dot(qq.astype(BF16), kb) * d).astype(BF16), q, kbf, decay)
        qwo = each(_dot, qk, wu)
        g_last = each(lambda t: exact_dot(last_bf, t), gsum)
        k_dec = each(lambda kk, gl, gs: (kk * jnp.exp(gl - gs)).astype(BF16), k, g_last, gsum)
        g_tot = each(jnp.exp, g_last)
        q_eff = each(lambda qq, e, t: (qq * e - t[:, :D]).astype(BF16), q, e_g, qwo)
        for p in heads:
            oc_s[p, pl.ds(r0, R), :] = qwo[p][:, D:]
        for u in range(U):
            c = sc * U + u
            rows = slice(u * C, (u + 1) * C)
            an = each(lambda kd, w: lax.dot_general(kd[rows], w[rows], (((0,), (0,)), ((), ())),
                                                    preferred_element_type=F32), k_dec, wu)
            for p in heads:
                base = pl.multiple_of(c * (P * AQ) + p * AQ, SUBLANES)
                aq_s[pl.ds(base, D), :] = an[p][:, :D].astype(BF16)
                aq_s[pl.ds(base + D, C), :] = q_eff[p][rows]
                n_s[p, pl.ds(pl.multiple_of(c * D, D), D), :] = an[p][:, D:]
                gt_s[p, pl.ds(pl.multiple_of(c * SUBLANES, SUBLANES), SUBLANES), :] = (
                    g_tot[p][u * C:u * C + SUBLANES])
        return carry

    lax.fori_loop(0, n_chunks // U, prep, 0)

    gain = gn_ref[...]

    def scan(c, s):
        r0 = pl.multiple_of(c * C, C)
        z = z_ref[pl.ds(r0, C), :]
        r = _dot(aq_s[pl.ds(pl.multiple_of(c * (P * AQ), SUBLANES), P * AQ), :], s.astype(BF16))
        new_s = []
        outs = []
        for p in range(P):
            rp = r[p * AQ:(p + 1) * AQ, p * D:(p + 1) * D]
            g_tot = gt_s[p, pl.ds(pl.multiple_of(c * SUBLANES, SUBLANES), 1), :]
            new_s.append(s[:, p * D:(p + 1) * D] * g_tot - rp[:D]
                         + n_s[p, pl.ds(pl.multiple_of(c * D, D), D), :])
            o = rp[D:] + oc_s[p, pl.ds(r0, C), :]
            ms = jnp.mean(o * o, axis=-1, keepdims=True)
            y = o * lax.rsqrt(ms + RMS_EPS) * gain
            outs.append(y * _silu(z[:, p * D:(p + 1) * D]))
        o_ref[pl.ds(r0, C), :] = jnp.concatenate(outs, axis=1).astype(o_ref.dtype)
        return jnp.concatenate(new_s, axis=1)

    final = lax.fori_loop(0, n_chunks, scan, jnp.concatenate([s0_ref[p] for p in range(P)], axis=1))
    for p in range(P):
        sn_ref[p] = final[:, p * D:(p + 1) * D]


def _delta_branch(proj1, ba, conv_buf8, conv_w, s0, a_log, dt_bias, out_gain, *, batch, seq, valid, n_heads):
    D = DN_HEAD_DIM
    C = DN_CHUNK
    H = n_heads
    P = DN_HEADS_PER_STEP
    G = H // P
    PD = P * D
    assert H % P == 0
    n_chunks = seq // C
    assert n_chunks % min(DN_SUPER, n_chunks) == 0
    blk = lambda off: pl.BlockSpec((seq, PD), lambda b, h: (b, off + h))
    cbs = lambda off: pl.BlockSpec((None, CONV_PAD, PD), lambda b, h: (b, 0, off + h))
    cws = lambda off: pl.BlockSpec((CONV_WIDTH, PD), lambda b, h: (0, off + h))
    smem = pl.BlockSpec(memory_space=pltpu.SMEM)
    o, s_new = pl.pallas_call(
        functools.partial(_delta_kernel, seq=seq, valid=valid, n_heads=H),
        grid=(batch, G),
        in_specs=[smem, smem,
                  blk(0), blk(G), blk(2 * G), blk(3 * G),
                  pl.BlockSpec((seq, LANES), lambda b, h: (b, 0)),
                  cbs(0), cbs(G), cbs(2 * G), cws(0), cws(G), cws(2 * G),
                  pl.BlockSpec((None, P, D, D), lambda b, h: (b, h, 0, 0)),
                  pl.BlockSpec((1, D), lambda b, h: (0, 0))],
        out_specs=[pl.BlockSpec((seq, PD), lambda b, h: (b, h)),
                   pl.BlockSpec((None, P, D, D), lambda b, h: (b, h, 0, 0))],
        out_shape=[jax.ShapeDtypeStruct((batch * seq, H * D), BF16),
                   jax.ShapeDtypeStruct((batch, H, D, D), F32)],
        scratch_shapes=[pltpu.VMEM((seq + CONV_PAD, PD), F32)] * 3 + [
            pltpu.VMEM((n_chunks * P * (D + C), D), BF16),
            pltpu.VMEM((P, n_chunks * D, D), F32),
            pltpu.VMEM((P, seq, D), F32),
            pltpu.VMEM((P, n_chunks * SUBLANES, LANES), F32)],
        compiler_params=_params(("arbitrary", "arbitrary")),
        name="delta",
    )(a_log, dt_bias, proj1, proj1, proj1, proj1, ba,
      conv_buf8, conv_buf8, conv_buf8, conv_w, conv_w, conv_w, s0, out_gain.reshape(1, D))
    return o, s_new


def _norm_rope(x, gain, cos, sin):
    ms = jnp.mean(x * x, axis=-1, keepdims=True)
    y = x * lax.rsqrt(ms + RMS_EPS) * gain
    return y * cos + pltpu.roll(y, DA_HEAD_DIM // 2, 1) * sin


def _lambda(lp, lam_init):
    return (jnp.exp(jnp.sum(lp[0:1] * lp[1:2], axis=1, keepdims=True))
            - jnp.exp(jnp.sum(lp[2:3] * lp[3:4], axis=1, keepdims=True)) + lam_init)


def _da_kernel(q_ref, k_ref, v_ref, cos_ref, sin_ref, qg_ref, kg_ref, sg_ref, lp_ref,
               o_ref, krow_ref, kbf_s, vbf_s, *, seq, tq, tk, lam_init):
    dh = DA_HEAD_DIM
    qi = pl.program_id(2)
    per = tq // tk
    scale = dh ** -0.5

    @pl.when(qi == 0)
    def _():
        kg = kg_ref[...]

        def body(c, carry):
            r0 = pl.multiple_of(c * tk, tk)
            kb = k_ref[pl.ds(r0, tk), :]
            cos = cos_ref[pl.ds(r0, tk), :]
            sin = sin_ref[pl.ds(r0, tk), :]
            kr = jnp.concatenate([_norm_rope(kb[:, :dh], kg, cos, sin),
                                  _norm_rope(kb[:, dh:], kg, cos, sin)], axis=1)
            krow_ref[pl.ds(r0, tk), :] = kr
            kbf_s[pl.ds(r0, tk), :] = kr.astype(BF16)
            vbf_s[pl.ds(r0, tk), :] = v_ref[pl.ds(r0, tk), :].astype(BF16)
            return carry

        lax.fori_loop(0, seq // tk, body, 0)

    q0 = pl.multiple_of(qi * tq, tq)
    qb = q_ref[...]
    cos = cos_ref[pl.ds(q0, tq), :]
    sin = sin_ref[pl.ds(q0, tq), :]
    qg = qg_ref[...]
    q1 = _norm_rope(qb[:, :dh], qg, cos, sin).astype(BF16)
    q2 = _norm_rope(qb[:, dh:], qg, cos, sin).astype(BF16)
    lam = _lambda(lp_ref[...], lam_init)
    q_pos = q0 + lax.broadcasted_iota(jnp.int32, (tq, tk), 0)
    k_off = lax.broadcasted_iota(jnp.int32, (tq, tk), 1)

    def scores(kt, masked):
        kb = kbf_s[pl.ds(pl.multiple_of(kt * tk, tk), tk), :]
        s1 = _nt_dot(q1, kb[:, :dh]) * scale
        s2 = _nt_dot(q2, kb[:, dh:]) * scale
        if masked:
            visible = kt * tk + k_off <= q_pos
            s1 = jnp.where(visible, s1, -jnp.inf)
            s2 = jnp.where(visible, s2, -jnp.inf)
        return s1, s2

    def fold(x, op):
        y = x[:, :LANES]
        for c in range(1, tk // LANES):
            y = op(y, x[:, c * LANES:(c + 1) * LANES])
        return y

    def over_tiles(step, carry):
        def group(j, c):
            for u in range(per):
                c = step(j * per + u, c, False)
            return c

        carry = lax.fori_loop(0, qi, group, carry)
        for u in range(per):
            carry = step(qi * per + u, carry, True)
        return carry

    def max_step(kt, carry, masked):
        s1, s2 = scores(kt, masked)
        return jnp.maximum(carry[0], fold(s1, jnp.maximum)), jnp.maximum(carry[1], fold(s2, jnp.maximum))

    neg = jnp.full((tq, LANES), -jnp.inf, F32)
    mx = over_tiles(max_step, (neg, neg))
    m1 = jnp.max(mx[0], axis=-1, keepdims=True)
    m2 = jnp.max(mx[1], axis=-1, keepdims=True)

    def sum_step(kt, carry, masked):
        l1, l2, a1, a2 = carry
        s1, s2 = scores(kt, masked)
        e1 = jnp.exp(s1 - m1)
        e2 = jnp.exp(s2 - m2)
        vb = vbf_s[pl.ds(pl.multiple_of(kt * tk, tk), tk), :]
        return (l1 + fold(e1, jnp.add), l2 + fold(e2, jnp.add),
                a1 + _dot(e1.astype(BF16), vb), a2 + _dot(e2.astype(BF16), vb))

    zl = jnp.zeros((tq, LANES), F32)
    za = jnp.zeros((tq, 2 * dh), F32)
    l1, l2, a1, a2 = over_tiles(sum_step, (zl, zl, za, za))
    inv1 = 1.0 / jnp.sum(l1, axis=-1, keepdims=True)
    inv2 = 1.0 / jnp.sum(l2, axis=-1, keepdims=True)
    o = a1 * inv1 - lam * (a2 * inv2)
    ms = jnp.mean(o * o, axis=-1, keepdims=True)
    o_ref[...] = (o * lax.rsqrt(ms + RMS_EPS) * sg_ref[...] * (1.0 - lam_init)).astype(o_ref.dtype)


def _diff_attn_prompt(proj3, cos, sin, q_gain, k_gain, sub_gain, lam_p, *, batch, seq, n_heads, lam_init, tq, tk):
    dh = DA_HEAD_DIM
    H = n_heads
    nq = seq // tq
    full = lambda shape: pl.BlockSpec(shape, lambda b, h, qi: (0, 0))
    o, k_rows = pl.pallas_call(
        functools.partial(_da_kernel, seq=seq, tq=tq, tk=tk, lam_init=lam_init),
        grid=(batch, H, nq),
        in_specs=[pl.BlockSpec((tq, 2 * dh), lambda b, h, qi: (b * nq + qi, h)),
                  pl.BlockSpec((seq, 2 * dh), lambda b, h, qi: (b, H + h)),
                  pl.BlockSpec((seq, 2 * dh), lambda b, h, qi: (b, 2 * H + h)),
                  full((seq, dh)), full((seq, dh)), full((1, dh)), full((1, dh)),
                  full((1, 2 * dh)), full((4, dh))],
        out_specs=[pl.BlockSpec((tq, 2 * dh), lambda b, h, qi: (b * nq + qi, h)),
                   pl.BlockSpec((seq, 2 * dh), lambda b, h, qi: (b, h))],
        out_shape=[jax.ShapeDtypeStruct((batch * seq, H * 2 * dh), BF16),
                   jax.ShapeDtypeStruct((batch * seq, H * 2 * dh), F32)],
        scratch_shapes=[pltpu.VMEM((seq, 2 * dh), BF16), pltpu.VMEM((seq, 2 * dh), BF16)],
        compiler_params=_params(("arbitrary", "arbitrary", "arbitrary")),
        name="diff_attn_prompt",
    )(proj3, proj3, proj3, cos, sin, q_gain.reshape(1, dh), k_gain.reshape(1, dh),
      sub_gain.reshape(1, 2 * dh), lam_p)
    return o, k_rows


def _decode_kernel(pt_ref, q_ref, kn_ref, vn_ref, cos_ref, sin_ref, qg_ref, kg_ref, sg_ref, lp_ref,
                   *refs, n_steps, n_heads, lam_init):
    G = PAGES_PER_STEP
    kc_refs, vc_refs = refs[:G], refs[G:2 * G]
    o_ref, kout_ref, qmat_s, sc_s, a_s, acc_s, new_s = refs[2 * G:]
    dh = DA_HEAD_DIM
    H = n_heads
    R = 2 * H
    page = kc_refs[0].shape[0]
    PH = page * H
    p = pl.program_id(1)
    scale = dh ** -0.5
    own_head = (lax.broadcasted_iota(jnp.int32, (R, PH), 1) % H
                == lax.broadcasted_iota(jnp.int32, (R, PH), 0) % H)

    @pl.when(p == 0)
    def _():
        cos = cos_ref[...]
        sin = sin_ref[...]
        qn = _norm_rope(q_ref[...], qg_ref[...], cos, sin)
        kn = _norm_rope(kn_ref[...], kg_ref[...], cos, sin)
        kout_ref[...] = kn
        sub = lax.broadcasted_iota(jnp.int32, (R, 2 * dh), 0) // H
        half = lax.broadcasted_iota(jnp.int32, (R, 2 * dh), 1) // dh
        qmat_s[...] = jnp.where(sub == half, jnp.concatenate([qn, qn], axis=1), 0.0).astype(BF16)
        s_new = jnp.sum(qn.astype(BF16).astype(F32) * kn.astype(BF16).astype(F32),
                        axis=-1, keepdims=True) * scale
        new_s[...] = jnp.broadcast_to(s_new, (R, LANES))
        acc_s[...] = jnp.zeros_like(acc_s)

    @pl.when(p < n_steps)
    def _():
        for g in range(G):
            kp = kc_refs[g][...].reshape(PH, 2 * dh).astype(BF16)
            s = _nt_dot(qmat_s[...], kp) * scale
            sc_s[p * G + g] = jnp.where(own_head, s, -jnp.inf)

    @pl.when(p == n_steps - 1)
    def _():
        n_pages = n_steps * G
        s_new = new_s[...]
        unroll = 8 if n_pages % 8 == 0 else 1

        def pages(fn, init):
            def group(j, acc):
                for u in range(unroll):
                    acc = fn(j * unroll + u, acc)
                return acc
            return lax.fori_loop(0, n_pages // unroll, group, init)

        m_el = pages(lambda i, acc: jnp.maximum(acc, sc_s[i]), jnp.full((R, PH), -jnp.inf, F32))
        m = jnp.maximum(jnp.max(m_el, axis=-1, keepdims=True), s_new[:, 0:1])
        l_el = pages(lambda i, acc: acc + jnp.exp(sc_s[i] - m), jnp.zeros((R, PH), F32))
        e_new = jnp.exp(s_new - m)
        l = jnp.sum(l_el, axis=-1, keepdims=True) + e_new[:, 0:1]
        inv = 1.0 / l
        lam = _lambda(lp_ref[...], lam_init)

        def weights(i, carry):
            pr = jnp.exp(sc_s[i] - m) * inv
            a = pr[:H] - lam * pr[H:]
            a_s[i] = jnp.concatenate([a, jnp.zeros_like(a)], axis=0).astype(BF16)
            return carry

        pages(weights, 0)
        pn = e_new * inv
        new_s[...] = jnp.concatenate([pn[:H] - lam * pn[H:], jnp.zeros((R - H, LANES), F32)], axis=0)

    @pl.when(p >= n_steps)
    def _():
        acc = acc_s[...]
        for g in range(G):
            vp = vc_refs[g][...].reshape(PH, 2 * dh).astype(BF16)
            acc = acc + _dot(a_s[(p - n_steps) * G + g], vp)
        acc_s[...] = acc

    @pl.when(p == 2 * n_steps - 1)
    def _():
        a_new = new_s[...][:H, 0:1].astype(BF16).astype(F32)
        o = acc_s[...][:H] + a_new * vn_ref[...].astype(BF16).astype(F32)
        ms = jnp.mean(o * o, axis=-1, keepdims=True)
        o_ref[...] = (o * lax.rsqrt(ms + RMS_EPS) * sg_ref[...] * (1.0 - lam_init)).astype(o_ref.dtype)


def _diff_attn_sample(page_table, q_sh, k_sh, v_new, cos, sin, q_gain, k_gain, sub_gain, lam_p,
                      cache_k, cache_v, *, layer, lam_init):
    dh = DA_HEAD_DIM
    G = PAGES_PER_STEP
    B, n_pages = page_table.shape
    H = v_new.shape[1]
    R = 2 * H
    page = cache_k.shape[2]
    assert n_pages % G == 0
    n_steps = n_pages // G
    per_b = lambda shape: pl.BlockSpec((None,) + shape, lambda b, p, pt: (b, 0, 0))
    full = lambda shape: pl.BlockSpec(shape, lambda b, p, pt: (0, 0))

    def k_page(g):
        return pl.BlockSpec((None, None, page, H, 2 * dh),
                            lambda b, p, pt: (layer, pt[b, jnp.minimum(p, n_steps - 1) * G + g], 0, 0, 0))

    def v_page(g):
        return pl.BlockSpec((None, None, page, H, 2 * dh),
                            lambda b, p, pt: (layer, pt[b, jnp.maximum(p - n_steps, 0) * G + g], 0, 0, 0))

    grid_spec = pltpu.PrefetchScalarGridSpec(
        num_scalar_prefetch=1,
        grid=(B, 2 * n_steps),
        in_specs=[per_b((R, dh)), per_b((R, dh)), per_b((H, 2 * dh)),
                  full((1, dh)), full((1, dh)), full((1, dh)), full((1, dh)),
                  full((1, 2 * dh)), full((4, dh))]
                 + [k_page(g) for g in range(G)] + [v_page(g) for g in range(G)],
        out_specs=[per_b((H, 2 * dh)), per_b((R, dh))],
        scratch_shapes=[pltpu.VMEM((R, 2 * dh), BF16),
                        pltpu.VMEM((n_pages, R, page * H), F32),
                        pltpu.VMEM((n_pages, R, page * H), BF16),
                        pltpu.VMEM((R, 2 * dh), F32),
                        pltpu.VMEM((R, LANES), F32)])
    o, k_out = pl.pallas_call(
        functools.partial(_decode_kernel, n_steps=n_steps, n_heads=H, lam_init=lam_init),
        grid_spec=grid_spec,
        out_shape=[jax.ShapeDtypeStruct((B, H, 2 * dh), BF16),
                   jax.ShapeDtypeStruct((B, R, dh), F32)],
        compiler_params=_params(("arbitrary", "arbitrary")),
        name="diff_attn_sample",
    )(page_table, q_sh, k_sh, v_new, cos, sin, q_gain.reshape(1, dh), k_gain.reshape(1, dh),
      sub_gain.reshape(1, 2 * dh), lam_p, *([cache_k] * G), *([cache_v] * G))
    return o, k_out


def _rope_tables(pos):
    half = DA_HEAD_DIM // 2
    inv_freq = ROPE_THETA ** (-jnp.arange(half, dtype=F32) / half)
    ang = pos.astype(F32)[:, None] * inv_freq[None, :]
    cos = jnp.cos(ang)
    sin = jnp.sin(ang)
    return jnp.concatenate([cos, cos], axis=1), jnp.concatenate([-sin, sin], axis=1)


def _pad_tail(t):
    return jnp.concatenate([t, jnp.zeros((TAIL - t.shape[0],) + t.shape[1:], t.dtype)], axis=0)


def kernel(x_prompt, x_sample, cache_k, cache_v, state_delta, state_conv, page_table, attn_norm, w_in, dn_conv, dn_a_log, dn_dt_bias, dn_out_norm, da_q_norm, da_k_norm, da_lambda, da_sub_norm, w_branch_a, w_branch_b, w_out, ffn_norm, w_gate_up, w_down):
    B, L, D = x_prompt.shape
    DB = x_sample.shape[0]
    depth = w_in.shape[0]
    assert x_sample.shape[1] == 1 and DB <= TAIL and L % DN_CHUNK == 0
    n_pages = page_table.shape[1]
    page = cache_k.shape[2]
    past_len = n_pages * page
    dn_w = dn_conv.shape[2] // 3
    dn_h = dn_w // DN_HEAD_DIM
    da_h = cache_k.shape[3]
    da_w = da_h * 2 * DA_HEAD_DIM
    d_ff = w_down.shape[1]
    off_beta = 4 * dn_w
    shift = 2 * dn_h
    n3 = 3 * da_w + 2 * D
    assert shift < LANES and w_in.shape[2] == off_beta + shift + n3

    m = B * L
    tm = _pick_tile(m, 1024, unit=TAIL)
    x = x_prompt.reshape(m, D)
    xs = _pad_tail(x_sample.reshape(DB, D))
    cos_p, sin_p = _rope_tables(jnp.arange(L))
    cos_s, sin_s = _rope_tables(past_len + jnp.arange(1))
    tq = _pick_tile(L, 512)
    tk = _pick_tile(tq, 256)
    zero_buf = jnp.zeros((B, CONV_PAD, 3 * dn_w), F32)
    zero_state = jnp.zeros((B, dn_h, DN_HEAD_DIM, DN_HEAD_DIM), F32)
    seq_s = DN_CHUNK
    tn3 = _pick_tile(math.gcd(n3, off_beta), 512)

    kp, vp, ksm, vsm, sp, ssm, cp, csm = [], [], [], [], [], [], [], []
    for l in range(depth):
        lam_init = 0.8 - 0.6 * math.exp(-0.3 * l)
        h, hs = _rmsnorm(x, xs, attn_norm[l])
        proj1, proj1_s = _matmul(h, hs, w_in[l], n_cols=off_beta, tn=_pick_tile(off_beta, 512), tm=tm,
                                 out_dtype=F32, name="mm_proj_dn")
        ba, ba_s = _matmul(h, hs, w_in[l], n_cols=LANES, tn=LANES, tm=tm, out_dtype=F32,
                           b_col0=off_beta, name="mm_proj_ba")
        proj3, proj3_s = _matmul(h, hs, w_in[l], n_cols=n3, tn=tn3, tm=tm, out_dtype=F32,
                                 b_col0=off_beta, shift=shift, name="mm_proj_da")

        o_dn, s_p = _delta_branch(proj1, ba, zero_buf, dn_conv[l], zero_state, dn_a_log[l], dn_dt_bias[l],
                                  dn_out_norm[l], batch=B, seq=L, valid=L, n_heads=dn_h)
        spread = lambda t: jnp.zeros((DB, seq_s, t.shape[1]), F32).at[:, 0].set(t[:DB]).reshape(DB * seq_s, -1)
        buf_s = jnp.concatenate([jnp.zeros((DB, CONV_PAD - (CONV_WIDTH - 1), 3 * dn_w), F32),
                                 state_conv[l]], axis=1)
        o_dn_sq, s_s = _delta_branch(spread(proj1_s), spread(ba_s), buf_s, dn_conv[l], state_delta[l],
                                     dn_a_log[l], dn_dt_bias[l], dn_out_norm[l],
                                     batch=DB, seq=seq_s, valid=1, n_heads=dn_h)
        o_dn_s = _pad_tail(o_dn_sq.reshape(DB, seq_s, dn_w)[:, 0])
        sp.append(s_p)
        ssm.append(s_s)
        cp.append(proj1[:, :3 * dn_w].reshape(B, L, 3 * dn_w)[:, L - (CONV_WIDTH - 1):])
        csm.append(jnp.concatenate([state_conv[l][:, 1:], proj1_s[:DB, None, :3 * dn_w]], axis=1))

        o_da, k_rows_p = _diff_attn_prompt(proj3, cos_p, sin_p, da_q_norm[l], da_k_norm[l],
                                           da_sub_norm[l], da_lambda[l], batch=B, seq=L,
                                           n_heads=da_h, lam_init=lam_init, tq=tq, tk=tk)
        tail3 = proj3_s[:DB]
        to_sh = lambda t: t.reshape(DB, da_h, 2, DA_HEAD_DIM).transpose(0, 2, 1, 3).reshape(
            DB, 2 * da_h, DA_HEAD_DIM)
        v_new = tail3[:, 2 * da_w:3 * da_w].reshape(DB, da_h, 2 * DA_HEAD_DIM)
        o_da_sq, k_new = _diff_attn_sample(page_table, to_sh(tail3[:, :da_w]), to_sh(tail3[:, da_w:2 * da_w]),
                                           v_new, cos_s, sin_s, da_q_norm[l], da_k_norm[l], da_sub_norm[l],
                                           da_lambda[l], cache_k, cache_v, layer=l, lam_init=lam_init)
        o_da_s = _pad_tail(o_da_sq.reshape(DB, da_w))
        kp.append(k_rows_p.reshape(B, L, da_h, 2 * DA_HEAD_DIM))
        vp.append(proj3[:, 2 * da_w:3 * da_w].reshape(B, L, da_h, 2 * DA_HEAD_DIM))
        ksm.append(k_new.reshape(DB, 2, da_h, DA_HEAD_DIM).transpose(0, 2, 1, 3).reshape(
            DB, 1, da_h, 2 * DA_HEAD_DIM))
        vsm.append(v_new.reshape(DB, 1, da_h, 2 * DA_HEAD_DIM))

        tnd = _pick_tile(D, 512)
        merged, merged_s = _matmul_merge(o_dn, o_da, o_dn_s, o_da_s, w_branch_a[l], w_branch_b[l],
                                         proj3, proj3_s, gate_col0=3 * da_w, tn=tnd, tm=tm)
        x, xs = _matmul(merged, merged_s, w_out[l], n_cols=D, tn=tnd, tm=tm, out_dtype=F32,
                        resid=(x, xs), name="mm_out")
        hn, hns = _rmsnorm(x, xs, ffn_norm[l])
        act, act_s = _matmul_swiglu(hn, hns, w_gate_up[l], d_ff=d_ff, tn=_pick_tile(d_ff, 256), tm=tm)
        x, xs = _matmul(act, act_s, w_down[l].astype(BF16), n_cols=D, tn=tnd,
                        tm=_pick_tile(m, 512, unit=TAIL), out_dtype=F32, resid=(x, xs), name="mm_down")

    y_prompt = x.reshape(B, L, D)
    y_sample = xs[:DB].reshape(DB, 1, D)
    return (y_prompt, y_sample, jnp.stack(kp), jnp.stack(vp), jnp.stack(ksm), jnp.stack(vsm),
            jnp.stack(sp), jnp.stack(ssm), jnp.stack(cp), jnp.stack(csm))
```

```python
import functools
import math

import jax
import jax.numpy as jnp
from jax import lax
from jax.experimental import pallas as pl
from jax.experimental.pallas import tpu as pltpu

F32 = jnp.float32
BF16 = jnp.bfloat16

RMS_EPS = 1e-6
ROPE_THETA = 10000.0
CONV_WIDTH = 4
DN_HEAD_DIM = 128
DN_CHUNK = 64
DA_HEAD_DIM = 128
LANES = 128
SUBLANES = 8
TAIL = 16
CONV_PAD = 8
VMEM_LIMIT = 56 * 1024 * 1024
DN_HEADS_PER_STEP = 2
PAGES_PER_STEP = 8


def _pick_tile(n, cap, unit=LANES):
    best = None
    t = unit
    while t <= min(n, cap):
        if n % t == 0:
            best = t
        t += unit
    assert best is not None, (n, cap)
    return best


def _nt_dot(a, b):
    return lax.dot_general(a, b, (((1,), (1,)), ((), ())), preferred_element_type=F32)


def _dot(a, b):
    return jnp.dot(a, b, preferred_element_type=F32)


def _silu(x):
    return x * jax.nn.sigmoid(x)


def _split3(x):
    hi = x.astype(BF16)
    r = x - hi.astype(F32)
    mid = r.astype(BF16)
    return hi, mid, (r - mid.astype(F32)).astype(BF16)


def _params(sem):
    return pltpu.CompilerParams(dimension_semantics=sem, vmem_limit_bytes=VMEM_LIMIT)


def _rmsnorm_kernel(x_ref, xs_ref, g_ref, o_ref, os_ref):
    def norm(x):
        ms = jnp.mean(x * x, axis=-1, keepdims=True)
        return (x * lax.rsqrt(ms + RMS_EPS) * g_ref[...]).astype(BF16)

    o_ref[...] = norm(x_ref[...])

    @pl.when(pl.program_id(0) == 0)
    def _():
        os_ref[...] = norm(xs_ref[...])


def _rmsnorm(x, xs, gain, tr=256):
    m, d = x.shape
    return pl.pallas_call(
        _rmsnorm_kernel,
        grid=(m // tr,),
        in_specs=[pl.BlockSpec((tr, d), lambda i: (i, 0)),
                  pl.BlockSpec((TAIL, d), lambda i: (0, 0)),
                  pl.BlockSpec((1, d), lambda i: (0, 0))],
        out_specs=[pl.BlockSpec((tr, d), lambda i: (i, 0)),
                   pl.BlockSpec((TAIL, d), lambda i: (0, 0))],
        out_shape=[jax.ShapeDtypeStruct((m, d), BF16), jax.ShapeDtypeStruct((TAIL, d), BF16)],
        compiler_params=_params(("arbitrary",)),
        name="rmsnorm",
    )(x, xs, gain.reshape(1, d))


CAST_ROWS = 256


def _cast_weight(b_ref, bscr, b_next=None, shift=0):
    k, tn = bscr.shape
    step = CAST_ROWS if k % CAST_ROWS == 0 else k

    def body(c, carry):
        r0 = pl.multiple_of(c * step, step)
        w = b_ref[pl.ds(r0, step), :]
        if b_next is not None:
            w = jnp.concatenate([w, b_next[pl.ds(r0, step), :]], axis=1)[:, shift:shift + tn]
        bscr[pl.ds(r0, step), :] = w.astype(BF16)
        return carry

    lax.fori_loop(0, k // step, body, 0)


def _mm_plain_kernel(*refs, shift, cast, resid):
    refs = list(refs)
    a_ref, as_ref, b_ref = refs[:3]
    del refs[:3]
    bn_ref = refs.pop(0) if shift else None
    r_ref, rs_ref = (refs.pop(0), refs.pop(0)) if resid else (None, None)
    o_ref, os_ref = refs[:2]
    w = refs[2] if cast else b_ref
    first = pl.program_id(1) == 0

    if cast:
        @pl.when(first)
        def _():
            _cast_weight(b_ref, w, bn_ref, shift)

    def out(a, r):
        y = _dot(a[...], w[...])
        return y if r is None else r[...] + y

    o_ref[...] = out(a_ref, r_ref).astype(o_ref.dtype)

    @pl.when(first)
    def _():
        os_ref[...] = out(as_ref, rs_ref).astype(os_ref.dtype)


def _matmul(a, a_s, b, layer, *, n_cols, tn, tm, out_dtype, b_col0=0, shift=0, resid=None, name="mm"):
    m, k = a.shape
    assert m % tm == 0 and b_col0 % tn == 0 and n_cols % tn == 0 and 0 <= shift < LANES
    joff = b_col0 // tn
    cast = b.dtype != BF16
    assert cast or not shift
    in_specs = [pl.BlockSpec((tm, k), lambda j, i: (i, 0)),
                pl.BlockSpec((TAIL, k), lambda j, i: (0, 0)),
                pl.BlockSpec((None, k, tn), lambda j, i: (layer, 0, j + joff))]
    args = [a, a_s, b]
    if shift:
        per = tn // LANES
        in_specs.append(pl.BlockSpec((None, k, LANES), lambda j, i: (layer, 0, (j + joff + 1) * per)))
        args.append(b)
    if resid is not None:
        in_specs += [pl.BlockSpec((tm, tn), lambda j, i: (i, j)),
                     pl.BlockSpec((TAIL, tn), lambda j, i: (0, j))]
        args += list(resid)
    return pl.pallas_call(
        functools.partial(_mm_plain_kernel, shift=shift, cast=cast, resid=resid is not None),
        grid=(n_cols // tn, m // tm),
        in_specs=in_specs,
        out_specs=[pl.BlockSpec((tm, tn), lambda j, i: (i, j)),
                   pl.BlockSpec((TAIL, tn), lambda j, i: (0, j))],
        out_shape=[jax.ShapeDtypeStruct((m, n_cols), out_dtype),
                   jax.ShapeDtypeStruct((TAIL, n_cols), out_dtype)],
        scratch_shapes=[pltpu.VMEM((k, tn), BF16)] if cast else [],
        compiler_params=_params(("arbitrary", "arbitrary")),
        name=name,
    )(*args)


def _mm_swiglu_kernel(a_ref, as_ref, bg_ref, bu_ref, o_ref, os_ref, sg, su):
    first = pl.program_id(1) == 0

    @pl.when(first)
    def _():
        _cast_weight(bg_ref, sg)
        _cast_weight(bu_ref, su)

    def out(a):
        return (_silu(_dot(a, sg[...])) * _dot(a, su[...])).astype(BF16)

    o_ref[...] = out(a_ref[...])

    @pl.when(first)
    def _():
        os_ref[...] = out(as_ref[...])


def _matmul_swiglu(a, a_s, w_gate_up, layer, *, d_ff, tn, tm):
    m, k = a.shape
    uoff = d_ff // tn
    return pl.pallas_call(
        _mm_swiglu_kernel,
        grid=(d_ff // tn, m // tm),
        in_specs=[pl.BlockSpec((tm, k), lambda j, i: (i, 0)),
                  pl.BlockSpec((TAIL, k), lambda j, i: (0, 0)),
                  pl.BlockSpec((None, k, tn), lambda j, i: (layer, 0, j)),
                  pl.BlockSpec((None, k, tn), lambda j, i: (layer, 0, j + uoff))],
        out_specs=[pl.BlockSpec((tm, tn), lambda j, i: (i, j)),
                   pl.BlockSpec((TAIL, tn), lambda j, i: (0, j))],
        out_shape=[jax.ShapeDtypeStruct((m, d_ff), BF16), jax.ShapeDtypeStruct((TAIL, d_ff), BF16)],
        scratch_shapes=[pltpu.VMEM((k, tn), BF16), pltpu.VMEM((k, tn), BF16)],
        compiler_params=_params(("arbitrary", "arbitrary")),
        name="mm_swiglu",
    )(a, a_s, w_gate_up, w_gate_up)


def _mm_merge_kernel(a1_ref, a2_ref, a1s_ref, a2s_ref, b1_ref, b2_ref, g1_ref, g2_ref, g1s_ref, g2s_ref,
                     o_ref, os_ref, s1, s2):
    first = pl.program_id(1) == 0

    @pl.when(first)
    def _():
        _cast_weight(b1_ref, s1)
        _cast_weight(b2_ref, s2)

    def out(a1, a2, g1, g2):
        return (jax.nn.sigmoid(g1[...]) * _dot(a1[...], s1[...])
                + jax.nn.sigmoid(g2[...]) * _dot(a2[...], s2[...])).astype(BF16)

    o_ref[...] = out(a1_ref, a2_ref, g1_ref, g2_ref)

    @pl.when(first)
    def _():
        os_ref[...] = out(a1s_ref, a2s_ref, g1s_ref, g2s_ref)


def _matmul_merge(a1, a2, a1s, a2s, b1, b2, layer, gates, gates_s, *, gate_col0, tn, tm):
    m, k = a1.shape
    n = b1.shape[2]
    assert gate_col0 % tn == 0
    g1off = gate_col0 // tn
    g2off = g1off + n // tn
    row = lambda blk: pl.BlockSpec((tm, blk), lambda j, i: (i, 0))
    tail = lambda blk: pl.BlockSpec((TAIL, blk), lambda j, i: (0, 0))
    wt = pl.BlockSpec((None, k, tn), lambda j, i: (layer, 0, j))
    return pl.pallas_call(
        _mm_merge_kernel,
        grid=(n // tn, m // tm),
        in_specs=[row(k), row(k), tail(k), tail(k), wt, wt,
                  pl.BlockSpec((tm, tn), lambda j, i: (i, j + g1off)),
                  pl.BlockSpec((tm, tn), lambda j, i: (i, j + g2off)),
                  pl.BlockSpec((TAIL, tn), lambda j, i: (0, j + g1off)),
                  pl.BlockSpec((TAIL, tn), lambda j, i: (0, j + g2off))],
        out_specs=[pl.BlockSpec((tm, tn), lambda j, i: (i, j)),
                   pl.BlockSpec((TAIL, tn), lambda j, i: (0, j))],
        out_shape=[jax.ShapeDtypeStruct((m, n), BF16), jax.ShapeDtypeStruct((TAIL, n), BF16)],
        scratch_shapes=[pltpu.VMEM((k, tn), BF16), pltpu.VMEM((k, tn), BF16)],
        compiler_params=_params(("arbitrary", "arbitrary")),
        name="mm_merge",
    )(a1, a2, a1s, a2s, b1, b2, gates, gates, gates_s, gates_s)


DN_SUPER = 4
DN_INTERLEAVE = 2


def _delta_kernel(alog_ref, dtb_ref,
                  qp_ref, kp_ref, vp_ref, z_ref, ba_ref,
                  cbq_ref, cbk_ref, cbv_ref, cwq_ref, cwk_ref, cwv_ref,
                  s0_ref, gn_ref,
                  o_ref, sn_ref,
                  xq_s, xk_s, xv_s, aq_s, n_s, oc_s, gt_s,
                  *, seq, valid, n_heads, single):
    C = DN_CHUNK
    D = DN_HEAD_DIM
    P = DN_HEADS_PER_STEP
    hp = pl.program_id(1)
    n_chunks = seq // C
    U = min(DN_SUPER, n_chunks)
    R = U * C
    AQ = D + C

    def rows(ref, r0, n):
        if not single:
            return ref[pl.ds(r0, n), :]
        assert n == seq
        first = lax.broadcasted_iota(jnp.int32, (seq, 1), 0) == 0
        return jnp.where(first, ref[pl.ds(pl.program_id(0), 1), :], 0.0)

    for xs, cb, xp in ((xq_s, cbq_ref, qp_ref), (xk_s, cbk_ref, kp_ref), (xv_s, cbv_ref, vp_ref)):
        xs[0:CONV_PAD, :] = cb[...]
        xs[CONV_PAD:, :] = rows(xp, 0, seq)

    row = lax.broadcasted_iota(jnp.int32, (R, R), 0)
    col = lax.broadcasted_iota(jnp.int32, (R, R), 1)
    same = (row // C) == (col // C)
    incl = same & (row >= col)
    strict = same & (row > col)
    diag = row == col
    eye = diag.astype(F32)
    tril_bf = incl.astype(BF16)
    ones_bf = jnp.ones((R, R), BF16)
    last_bf = (col == (row // C) * C + (C - 1)).astype(BF16)
    lane = lax.broadcasted_iota(jnp.int32, (R, LANES), 1)

    def conv_silu(xs, cw_ref, r0):
        xc = xs[pl.ds(r0, R + CONV_PAD), :]
        w = cw_ref[...]
        first = CONV_PAD - (CONV_WIDTH - 1)
        y = xc[first:first + R] * w[0:1]
        for t in range(1, CONV_WIDTH):
            y = y + xc[first + t:first + t + R] * w[t:t + 1]
        return _silu(y)

    def l2n(x):
        return x * lax.rsqrt(jnp.sum(x * x, axis=-1, keepdims=True) + RMS_EPS)

    def exact_dot(sel_bf, x):
        return sum(_dot(sel_bf, t) for t in _split3(x))

    def each(fn, *lists):
        return [fn(*args) for args in zip(*lists)]

    triu_same = same & (row <= col)

    S = DN_INTERLEAVE if (n_chunks // U) % DN_INTERLEAVE == 0 else 1
    units = [(s, p) for s in range(S) for p in range(P)]

    def prep(it, carry):
        r0s = [pl.multiple_of((it * S + s) * R, R) for s in range(S)]
        qc = [conv_silu(xq_s, cwq_ref, r0) for r0 in r0s]
        kc = [conv_silu(xk_s, cwk_ref, r0) for r0 in r0s]
        vc = [conv_silu(xv_s, cwv_ref, r0) for r0 in r0s]
        ba = [rows(ba_ref, r0, R) for r0 in r0s]
        q = [l2n(qc[s][:, p * D:(p + 1) * D]) * (D ** -0.5) for s, p in units]
        k = [l2n(kc[s][:, p * D:(p + 1) * D]) for s, p in units]
        v = [vc[s][:, p * D:(p + 1) * D] for s, p in units]
        beta, g = [], []
        for s, p in units:
            h = hp * P + p
            neg_a = -jnp.exp(jnp.full((1, 1), alog_ref[h], F32))
            dt_bias = jnp.full((1, 1), dtb_ref[h], F32)
            bcol = jnp.sum(jnp.where(lane == h, ba[s], 0.0), axis=-1, keepdims=True)
            acol = jnp.sum(jnp.where(lane == n_heads + h, ba[s], 0.0), axis=-1, keepdims=True)
            x = acol + dt_bias
            beta.append(jax.nn.sigmoid(bcol))
            g.append(neg_a * (jnp.maximum(x, 0.0) + jnp.log1p(jnp.exp(-jnp.abs(x)))))
        if valid < seq:
            live = [(r0s[s] + lax.broadcasted_iota(jnp.int32, (R, 1), 0)) < valid for s, _ in units]
            dead = lambda t, keep: jnp.where(keep, t, 0.0)
            k, v, beta, g = each(dead, k, live), each(dead, v, live), each(dead, beta, live), each(dead, g, live)

        g_wide = each(lambda t: jnp.broadcast_to(t, (R, max(R, D))), g)
        gsum = each(lambda t: exact_dot(tril_bf, t[:, :D]), g_wide)
        gj = each(lambda t: exact_dot(ones_bf, jnp.where(triu_same, t[:, :R], 0.0)), g_wide)
        gi = each(lambda t: jnp.concatenate([t] * (R // D), axis=1) if R > D else t[:, :R], gsum)
        decay = each(lambda a, b: jnp.exp(jnp.where(incl, a - b, -jnp.inf)), gi, gj)
        kbf = each(lambda t: t.astype(BF16), k)
        m = each(lambda kb, d, b: jnp.where(strict, _nt_dot(kb, kb) * d * b, 0.0), kbf, decay, beta)
        x_pow = each(lambda t: -t, m)
        t_inv = each(lambda t: eye + t, x_pow)
        for _ in range(int(math.log2(C)) - 1):
            xb = each(lambda t: t.astype(BF16), x_pow)
            x_pow = each(lambda t: _dot(t, t), xb)
            t_inv = each(lambda t, xp: t + _dot(t.astype(BF16), xp.astype(BF16)), t_inv, x_pow)
        m_hi = each(lambda t: t.astype(BF16), m)
        m_lo = each(lambda t, hi: (t - hi.astype(F32)).astype(BF16), m, m_hi)
        t_hi = each(lambda t: t.astype(BF16), t_inv)
        t_lo = each(lambda t, hi: (t - hi.astype(F32)).astype(BF16), t_inv, t_hi)
        resid = each(lambda t, mh, ml, th, tl: eye - t - (_dot(mh, th) + (_dot(mh, tl) + _dot(ml, th))),
                     t_inv, m_hi, m_lo, t_hi, t_lo)
        t_inv = each(lambda t, th, r: t + _dot(th, r.astype(BF16)), t_inv, t_hi, resid)

        e_g = each(jnp.exp, gsum)
        wu = each(lambda t, kk, vv, b, e: _dot(t.astype(BF16), jnp.concatenate(
            [kk * (b * e), vv * b], axis=1).astype(BF16)).astype(BF16), t_inv, k, v, beta, e_g)
        qk = each(lambda qq, kb, d: (_nt_dot(qq.astype(BF16), kb) * d).astype(BF16), q, kbf, decay)
        qwo = each(_dot, qk, wu)
        g_last = each(lambda t: exact_dot(last_bf, t), gsum)
        k_dec = each(lambda kk, gl, gs: (kk * jnp.exp(gl - gs)).astype(BF16), k, g_last, gsum)
        g_tot = each(jnp.exp, g_last)
        q_eff = each(lambda qq, e, t: (qq * e - t[:, :D]).astype(BF16), q, e_g, qwo)
        for i, (s, p) in enumerate(units):
            oc_s[p, pl.ds(r0s[s], R), :] = qwo[i][:, D:]
        for u in range(U):
            rws = slice(u * C, (u + 1) * C)
            an = each(lambda kd, w: lax.dot_general(kd[rws], w[rws], (((0,), (0,)), ((), ())),
                                                    preferred_element_type=F32), k_dec, wu)
            for i, (s, p) in enumerate(units):
                c = (it * S + s) * U + u
                base = pl.multiple_of(c * (P * AQ) + p * AQ, SUBLANES)
                aq_s[pl.ds(base, D), :] = an[i][:, :D].astype(BF16)
                aq_s[pl.ds(base + D, C), :] = q_eff[i][rws]
                n_s[p, pl.ds(pl.multiple_of(c * D, D), D), :] = an[i][:, D:]
                gt_s[p, pl.ds(pl.multiple_of(c * SUBLANES, SUBLANES), SUBLANES), :] = (
                    g_tot[i][u * C:u * C + SUBLANES])
        return carry

    lax.fori_loop(0, n_chunks // (U * S), prep, 0)

    gain = gn_ref[...]

    def scan(c, s):
        r0 = pl.multiple_of(c * C, C)
        z = rows(z_ref, r0, C)
        r = _dot(aq_s[pl.ds(pl.multiple_of(c * (P * AQ), SUBLANES), P * AQ), :], s.astype(BF16))
        new_s = []
        outs = []
        for p in range(P):
            rp = r[p * AQ:(p + 1) * AQ, p * D:(p + 1) * D]
            g_tot = gt_s[p, pl.ds(pl.multiple_of(c * SUBLANES, SUBLANES), 1), :]
            new_s.append(s[:, p * D:(p + 1) * D] * g_tot - rp[:D]
                         + n_s[p, pl.ds(pl.multiple_of(c * D, D), D), :])
            o = rp[D:] + oc_s[p, pl.ds(r0, C), :]
            ms = jnp.mean(o * o, axis=-1, keepdims=True)
            y = o * lax.rsqrt(ms + RMS_EPS) * gain
            outs.append(y * _silu(z[:, p * D:(p + 1) * D]))
        o_ref[pl.ds(r0, C), :] = jnp.concatenate(outs, axis=1).astype(o_ref.dtype)
        return jnp.concatenate(new_s, axis=1)

    final = lax.fori_loop(0, n_chunks, scan, jnp.concatenate([s0_ref[p] for p in range(P)], axis=1))
    for p in range(P):
        sn_ref[p] = final[:, p * D:(p + 1) * D]


def _delta_branch(proj1, ba, conv_buf8, conv_w, s0, a_log, dt_bias, out_gain, *, batch, seq, valid, n_heads,
                  single=False):
    D = DN_HEAD_DIM
    C = DN_CHUNK
    H = n_heads
    P = DN_HEADS_PER_STEP
    G = H // P
    PD = P * D
    assert H % P == 0
    n_chunks = seq // C
    assert n_chunks % min(DN_SUPER, n_chunks) == 0
    if single:
        assert valid == 1 and seq == C
        blk = lambda off: pl.BlockSpec((TAIL, PD), lambda b, h: (0, off + h))
        ba_spec = pl.BlockSpec((TAIL, LANES), lambda b, h: (0, 0))
    else:
        blk = lambda off: pl.BlockSpec((seq, PD), lambda b, h: (b, off + h))
        ba_spec = pl.BlockSpec((seq, LANES), lambda b, h: (b, 0))
    cbs = lambda off: pl.BlockSpec((None, CONV_PAD, PD), lambda b, h: (b, 0, off + h))
    cws = lambda off: pl.BlockSpec((CONV_WIDTH, PD), lambda b, h: (0, off + h))
    smem = pl.BlockSpec(memory_space=pltpu.SMEM)
    o, s_new = pl.pallas_call(
        functools.partial(_delta_kernel, seq=seq, valid=valid, n_heads=H, single=single),
        grid=(batch, G),
        in_specs=[smem, smem,
                  blk(0), blk(G), blk(2 * G), blk(3 * G), ba_spec,
                  cbs(0), cbs(G), cbs(2 * G), cws(0), cws(G), cws(2 * G),
                  pl.BlockSpec((None, P, D, D), lambda b, h: (b, h, 0, 0)),
                  pl.BlockSpec((1, D), lambda b, h: (0, 0))],
        out_specs=[pl.BlockSpec((seq, PD), lambda b, h: (b, h)),
                   pl.BlockSpec((None, P, D, D), lambda b, h: (b, h, 0, 0))],
        out_shape=[jax.ShapeDtypeStruct((batch * seq, H * D), BF16),
                   jax.ShapeDtypeStruct((batch, H, D, D), F32)],
        scratch_shapes=[pltpu.VMEM((seq + CONV_PAD, PD), F32)] * 3 + [
            pltpu.VMEM((n_chunks * P * (D + C), D), BF16),
            pltpu.VMEM((P, n_chunks * D, D), F32),
            pltpu.VMEM((P, seq, D), F32),
            pltpu.VMEM((P, n_chunks * SUBLANES, LANES), F32)],
        compiler_params=_params(("arbitrary", "arbitrary")),
        name="delta",
    )(a_log, dt_bias, proj1, proj1, proj1, proj1, ba,
      conv_buf8, conv_buf8, conv_buf8, conv_w, conv_w, conv_w, s0, out_gain.reshape(1, D))
    return o, s_new


def _norm_rope(x, gain, cos, sin):
    ms = jnp.mean(x * x, axis=-1, keepdims=True)
    y = x * lax.rsqrt(ms + RMS_EPS) * gain
    return y * cos + pltpu.roll(y, DA_HEAD_DIM // 2, 1) * sin


def _lambda(lp, lam_init):
    return (jnp.exp(jnp.sum(lp[0:1] * lp[1:2], axis=1, keepdims=True))
            - jnp.exp(jnp.sum(lp[2:3] * lp[3:4], axis=1, keepdims=True)) + lam_init)


def _da_kernel(q_ref, k_ref, v_ref, cos_ref, sin_ref, qg_ref, kg_ref, sg_ref, lp_ref,
               o_ref, krow_ref, vrow_ref, kbf_s, vbf_s, *, seq, tq, tk, lam_init):
    dh = DA_HEAD_DIM
    qi = pl.program_id(2)
    per = tq // tk
    scale = dh ** -0.5

    @pl.when(qi == 0)
    def _():
        kg = kg_ref[...]

        def body(c, carry):
            r0 = pl.multiple_of(c * tk, tk)
            kb = k_ref[pl.ds(r0, tk), :]
            cos = cos_ref[pl.ds(r0, tk), :]
            sin = sin_ref[pl.ds(r0, tk), :]
            kr = jnp.concatenate([_norm_rope(kb[:, :dh], kg, cos, sin),
                                  _norm_rope(kb[:, dh:], kg, cos, sin)], axis=1)
            krow_ref[pl.ds(r0, tk), :] = kr
            kbf_s[pl.ds(r0, tk), :] = kr.astype(BF16)
            vb = v_ref[pl.ds(r0, tk), :]
            vrow_ref[pl.ds(r0, tk), :] = vb
            vbf_s[pl.ds(r0, tk), :] = vb.astype(BF16)
            return carry

        lax.fori_loop(0, seq // tk, body, 0)

    q0 = pl.multiple_of(qi * tq, tq)
    qb = q_ref[...]
    cos = cos_ref[pl.ds(q0, tq), :]
    sin = sin_ref[pl.ds(q0, tq), :]
    qg = qg_ref[...]
    q1 = _norm_rope(qb[:, :dh], qg, cos, sin).astype(BF16)
    q2 = _norm_rope(qb[:, dh:], qg, cos, sin).astype(BF16)
    lam = _lambda(lp_ref[...], lam_init)
    q_pos = q0 + lax.broadcasted_iota(jnp.int32, (tq, tk), 0)
    k_off = lax.broadcasted_iota(jnp.int32, (tq, tk), 1)

    def scores(kt, masked):
        kb = kbf_s[pl.ds(pl.multiple_of(kt * tk, tk), tk), :]
        s1 = _nt_dot(q1, kb[:, :dh]) * scale
        s2 = _nt_dot(q2, kb[:, dh:]) * scale
        if masked:
            visible = kt * tk + k_off <= q_pos
            s1 = jnp.where(visible, s1, -jnp.inf)
            s2 = jnp.where(visible, s2, -jnp.inf)
        return s1, s2

    def fold(x, op):
        y = x[:, :LANES]
        for c in range(1, tk // LANES):
            y = op(y, x[:, c * LANES:(c + 1) * LANES])
        return y

    def over_tiles(step, carry):
        def group(j, c):
            for u in range(per):
                c = step(j * per + u, c, False)
            return c

        carry = lax.fori_loop(0, qi, group, carry)
        for u in range(per):
            carry = step(qi * per + u, carry, True)
        return carry

    def max_step(kt, carry, masked):
        s1, s2 = scores(kt, masked)
        return jnp.maximum(carry[0], fold(s1, jnp.maximum)), jnp.maximum(carry[1], fold(s2, jnp.maximum))

    neg = jnp.full((tq, LANES), -jnp.inf, F32)
    mx = over_tiles(max_step, (neg, neg))
    m1 = jnp.max(mx[0], axis=-1, keepdims=True)
    m2 = jnp.max(mx[1], axis=-1, keepdims=True)

    def sum_step(kt, carry, masked):
        l1, l2, a1, a2 = carry
        s1, s2 = scores(kt, masked)
        e1 = jnp.exp(s1 - m1)
        e2 = jnp.exp(s2 - m2)
        vb = vbf_s[pl.ds(pl.multiple_of(kt * tk, tk), tk), :]
        return (l1 + fold(e1, jnp.add), l2 + fold(e2, jnp.add),
                a1 + _dot(e1.astype(BF16), vb), a2 + _dot(e2.astype(BF16), vb))

    zl = jnp.zeros((tq, LANES), F32)
    za = jnp.zeros((tq, 2 * dh), F32)
    l1, l2, a1, a2 = over_tiles(sum_step, (zl, zl, za, za))
    inv1 = 1.0 / jnp.sum(l1, axis=-1, keepdims=True)
    inv2 = 1.0 / jnp.sum(l2, axis=-1, keepdims=True)
    o = a1 * inv1 - lam * (a2 * inv2)
    ms = jnp.mean(o * o, axis=-1, keepdims=True)
    o_ref[...] = (o * lax.rsqrt(ms + RMS_EPS) * sg_ref[...] * (1.0 - lam_init)).astype(o_ref.dtype)


def _diff_attn_prompt(proj3, cos, sin, q_gain, k_gain, sub_gain, lam_p, *, batch, seq, n_heads, lam_init, tq, tk):
    dh = DA_HEAD_DIM
    H = n_heads
    nq = seq // tq
    full = lambda shape: pl.BlockSpec(shape, lambda b, h, qi: (0, 0))
    return pl.pallas_call(
        functools.partial(_da_kernel, seq=seq, tq=tq, tk=tk, lam_init=lam_init),
        grid=(batch, H, nq),
        in_specs=[pl.BlockSpec((tq, 2 * dh), lambda b, h, qi: (b * nq + qi, h)),
                  pl.BlockSpec((seq, 2 * dh), lambda b, h, qi: (b, H + h)),
                  pl.BlockSpec((seq, 2 * dh), lambda b, h, qi: (b, 2 * H + h)),
                  full((seq, dh)), full((seq, dh)), full((1, dh)), full((1, dh)),
                  full((1, 2 * dh)), full((4, dh))],
        out_specs=[pl.BlockSpec((tq, 2 * dh), lambda b, h, qi: (b * nq + qi, h)),
                   pl.BlockSpec((seq, 2 * dh), lambda b, h, qi: (b, h)),
                   pl.BlockSpec((seq, 2 * dh), lambda b, h, qi: (b, h))],
        out_shape=[jax.ShapeDtypeStruct((batch * seq, H * 2 * dh), BF16),
                   jax.ShapeDtypeStruct((batch * seq, H * 2 * dh), F32),
                   jax.ShapeDtypeStruct((batch * seq, H * 2 * dh), F32)],
        scratch_shapes=[pltpu.VMEM((seq, 2 * dh), BF16), pltpu.VMEM((seq, 2 * dh), BF16)],
        compiler_params=_params(("arbitrary", "arbitrary", "arbitrary")),
        name="diff_attn_prompt",
    )(proj3, proj3, proj3, cos, sin, q_gain.reshape(1, dh), k_gain.reshape(1, dh),
      sub_gain.reshape(1, 2 * dh), lam_p)


def _decode_kernel(pt_ref, q_ref, kn_ref, vn_ref, cos_ref, sin_ref, qg_ref, kg_ref, sg_ref, lp_ref,
                   *refs, n_steps, pages_per_step, n_heads, lam_init):
    G = pages_per_step
    kc_refs, vc_refs = refs[:G], refs[G:2 * G]
    o_ref, kout_ref, qmat_s, sc_s, a_s, acc_s, new_s = refs[2 * G:]
    dh = DA_HEAD_DIM
    H = n_heads
    R = 2 * H
    page = kc_refs[0].shape[0]
    PH = page * H
    p = pl.program_id(1)
    scale = dh ** -0.5
    own_head = (lax.broadcasted_iota(jnp.int32, (R, PH), 1) % H
                == lax.broadcasted_iota(jnp.int32, (R, PH), 0) % H)

    @pl.when(p == 0)
    def _():
        cos = cos_ref[...]
        sin = sin_ref[...]
        qn = _norm_rope(q_ref[...], qg_ref[...], cos, sin)
        kn = _norm_rope(kn_ref[...], kg_ref[...], cos, sin)
        kout_ref[...] = kn
        sub = lax.broadcasted_iota(jnp.int32, (R, 2 * dh), 0) // H
        half = lax.broadcasted_iota(jnp.int32, (R, 2 * dh), 1) // dh
        qmat_s[...] = jnp.where(sub == half, jnp.concatenate([qn, qn], axis=1), 0.0).astype(BF16)
        s_new = jnp.sum(qn.astype(BF16).astype(F32) * kn.astype(BF16).astype(F32),
                        axis=-1, keepdims=True) * scale
        new_s[...] = jnp.broadcast_to(s_new, (R, LANES))
        acc_s[...] = jnp.zeros_like(acc_s)

    @pl.when(p < n_steps)
    def _():
        for g in range(G):
            kp = kc_refs[g][...].reshape(PH, 2 * dh).astype(BF16)
            s = _nt_dot(qmat_s[...], kp) * scale
            sc_s[p * G + g] = jnp.where(own_head, s, -jnp.inf)

    @pl.when(p == n_steps - 1)
    def _():
        n_pages = n_steps * G
        s_new = new_s[...]
        unroll = 8 if n_pages % 8 == 0 else 1

        def pages(fn, init):
            def group(j, acc):
                for u in range(unroll):
                    acc = fn(j * unroll + u, acc)
                return acc
            return lax.fori_loop(0, n_pages // unroll, group, init)

        m_el = pages(lambda i, acc: jnp.maximum(acc, sc_s[i]), jnp.full((R, PH), -jnp.inf, F32))
        m = jnp.maximum(jnp.max(m_el, axis=-1, keepdims=True), s_new[:, 0:1])
        l_el = pages(lambda i, acc: acc + jnp.exp(sc_s[i] - m), jnp.zeros((R, PH), F32))
        e_new = jnp.exp(s_new - m)
        l = jnp.sum(l_el, axis=-1, keepdims=True) + e_new[:, 0:1]
        inv = 1.0 / l
        lam = _lambda(lp_ref[...], lam_init)

        def weights(i, carry):
            pr = jnp.exp(sc_s[i] - m) * inv
            a = pr[:H] - lam * pr[H:]
            a_s[i] = jnp.concatenate([a, jnp.zeros_like(a)], axis=0).astype(BF16)
            return carry

        pages(weights, 0)
        pn = e_new * inv
        new_s[...] = jnp.concatenate([pn[:H] - lam * pn[H:], jnp.zeros((R - H, LANES), F32)], axis=0)

    @pl.when(p >= n_steps)
    def _():
        acc = acc_s[...]
        for g in range(G):
            vp = vc_refs[g][...].reshape(PH, 2 * dh).astype(BF16)
            acc = acc + _dot(a_s[(p - n_steps) * G + g], vp)
        acc_s[...] = acc

    @pl.when(p == 2 * n_steps - 1)
    def _():
        a_new = new_s[...][:H, 0:1].astype(BF16).astype(F32)
        o = acc_s[...][:H] + a_new * vn_ref[...].astype(BF16).astype(F32)
        ms = jnp.mean(o * o, axis=-1, keepdims=True)
        o_ref[...] = (o * lax.rsqrt(ms + RMS_EPS) * sg_ref[...] * (1.0 - lam_init)).astype(o_ref.dtype)


def _diff_attn_sample(page_table, q_sh, k_sh, v_new, cos, sin, q_gain, k_gain, sub_gain, lam_p,
                      cache_k, cache_v, *, layer, lam_init):
    dh = DA_HEAD_DIM
    B, n_pages = page_table.shape
    G = max(g for g in range(1, PAGES_PER_STEP + 1) if n_pages % g == 0)
    H = v_new.shape[1]
    R = 2 * H
    page = cache_k.shape[2]
    n_steps = n_pages // G
    per_b = lambda shape: pl.BlockSpec((None,) + shape, lambda b, p, pt: (b, 0, 0))
    full = lambda shape: pl.BlockSpec(shape, lambda b, p, pt: (0, 0))

    def k_page(g):
        return pl.BlockSpec((None, None, page, H, 2 * dh),
                            lambda b, p, pt: (layer, pt[b, jnp.minimum(p, n_steps - 1) * G + g], 0, 0, 0))

    def v_page(g):
        return pl.BlockSpec((None, None, page, H, 2 * dh),
                            lambda b, p, pt: (layer, pt[b, jnp.maximum(p - n_steps, 0) * G + g], 0, 0, 0))

    grid_spec = pltpu.PrefetchScalarGridSpec(
        num_scalar_prefetch=1,
        grid=(B, 2 * n_steps),
        in_specs=[per_b((R, dh)), per_b((R, dh)), per_b((H, 2 * dh)),
                  full((1, dh)), full((1, dh)), full((1, dh)), full((1, dh)),
                  full((1, 2 * dh)), full((4, dh))]
                 + [k_page(g) for g in range(G)] + [v_page(g) for g in range(G)],
        out_specs=[per_b((H, 2 * dh)), per_b((R, dh))],
        scratch_shapes=[pltpu.VMEM((R, 2 * dh), BF16),
                        pltpu.VMEM((n_pages, R, page * H), F32),
                        pltpu.VMEM((n_pages, R, page * H), BF16),
                        pltpu.VMEM((R, 2 * dh), F32),
                        pltpu.VMEM((R, LANES), F32)])
    o, k_out = pl.pallas_call(
        functools.partial(_decode_kernel, n_steps=n_steps, pages_per_step=G, n_heads=H, lam_init=lam_init),
        grid_spec=grid_spec,
        out_shape=[jax.ShapeDtypeStruct((B, H, 2 * dh), BF16),
                   jax.ShapeDtypeStruct((B, R, dh), F32)],
        compiler_params=_params(("arbitrary", "arbitrary")),
        name="diff_attn_sample",
    )(page_table, q_sh, k_sh, v_new, cos, sin, q_gain.reshape(1, dh), k_gain.reshape(1, dh),
      sub_gain.reshape(1, 2 * dh), lam_p, *([cache_k] * G), *([cache_v] * G))
    return o, k_out


def _rope_tables(pos):
    half = DA_HEAD_DIM // 2
    inv_freq = ROPE_THETA ** (-jnp.arange(half, dtype=F32) / half)
    ang = pos.astype(F32)[:, None] * inv_freq[None, :]
    cos = jnp.cos(ang)
    sin = jnp.sin(ang)
    return jnp.concatenate([cos, cos], axis=1), jnp.concatenate([-sin, sin], axis=1)


def _pad_tail(t):
    return jnp.concatenate([t, jnp.zeros((TAIL - t.shape[0],) + t.shape[1:], t.dtype)], axis=0)


def kernel(x_prompt, x_sample, cache_k, cache_v, state_delta, state_conv, page_table, attn_norm, w_in, dn_conv, dn_a_log, dn_dt_bias, dn_out_norm, da_q_norm, da_k_norm, da_lambda, da_sub_norm, w_branch_a, w_branch_b, w_out, ffn_norm, w_gate_up, w_down):
    B, L, D = x_prompt.shape
    DB = x_sample.shape[0]
    depth = w_in.shape[0]
    assert x_sample.shape[1] == 1 and DB <= TAIL and L % DN_CHUNK == 0
    n_pages = page_table.shape[1]
    page = cache_k.shape[2]
    past_len = n_pages * page
    dn_w = dn_conv.shape[2] // 3
    dn_h = dn_w // DN_HEAD_DIM
    da_h = cache_k.shape[3]
    da_w = da_h * 2 * DA_HEAD_DIM
    d_ff = w_down.shape[1]
    off_beta = 4 * dn_w
    shift = 2 * dn_h
    n3 = 3 * da_w + 2 * D
    assert shift < LANES and w_in.shape[2] == off_beta + shift + n3

    m = B * L
    tm = _pick_tile(m, 1024, unit=TAIL)
    x = x_prompt.reshape(m, D)
    xs = _pad_tail(x_sample.reshape(DB, D))
    cos_p, sin_p = _rope_tables(jnp.arange(L))
    cos_s, sin_s = _rope_tables(past_len + jnp.arange(1))
    tq = _pick_tile(L, 512)
    tk = _pick_tile(tq, 256)
    zero_buf = jnp.zeros((B, CONV_PAD, 3 * dn_w), F32)
    zero_state = jnp.zeros((B, dn_h, DN_HEAD_DIM, DN_HEAD_DIM), F32)
    seq_s = DN_CHUNK
    tn3 = _pick_tile(math.gcd(n3, off_beta), 512)
    w_down_bf = w_down.astype(BF16)

    kp, vp, ksm, vsm, sp, ssm, cp, csm = [], [], [], [], [], [], [], []
    for l in range(depth):
        lam_init = 0.8 - 0.6 * math.exp(-0.3 * l)
        h, hs = _rmsnorm(x, xs, attn_norm[l])
        proj1, proj1_s = _matmul(h, hs, w_in, l, n_cols=off_beta, tn=_pick_tile(off_beta, 512), tm=tm,
                                 out_dtype=F32, name="mm_proj_dn")
        ba, ba_s = _matmul(h, hs, w_in, l, n_cols=LANES, tn=LANES, tm=tm, out_dtype=F32,
                           b_col0=off_beta, name="mm_proj_ba")
        proj3, proj3_s = _matmul(h, hs, w_in, l, n_cols=n3, tn=tn3, tm=tm, out_dtype=F32,
                                 b_col0=off_beta, shift=shift, name="mm_proj_da")

        o_dn, s_p = _delta_branch(proj1, ba, zero_buf, dn_conv[l], zero_state, dn_a_log[l], dn_dt_bias[l],
                                  dn_out_norm[l], batch=B, seq=L, valid=L, n_heads=dn_h)
        buf_s = jnp.concatenate([jnp.zeros((DB, CONV_PAD - (CONV_WIDTH - 1), 3 * dn_w), F32),
                                 state_conv[l]], axis=1)
        o_dn_sq, s_s = _delta_branch(proj1_s, ba_s, buf_s, dn_conv[l], state_delta[l],
                                     dn_a_log[l], dn_dt_bias[l], dn_out_norm[l],
                                     batch=DB, seq=seq_s, valid=1, n_heads=dn_h, single=True)
        o_dn_s = _pad_tail(o_dn_sq.reshape(DB, seq_s, dn_w)[:, 0])
        sp.append(s_p)
        ssm.append(s_s)
        cp.append(proj1[:, :3 * dn_w].reshape(B, L, 3 * dn_w)[:, L - (CONV_WIDTH - 1):])
        csm.append(jnp.concatenate([state_conv[l][:, 1:], proj1_s[:DB, None, :3 * dn_w]], axis=1))

        o_da, k_rows_p, v_rows_p = _diff_attn_prompt(proj3, cos_p, sin_p, da_q_norm[l], da_k_norm[l],
                                           da_sub_norm[l], da_lambda[l], batch=B, seq=L,
                                           n_heads=da_h, lam_init=lam_init, tq=tq, tk=tk)
        tail3 = proj3_s[:DB]
        to_sh = lambda t: t.reshape(DB, da_h, 2, DA_HEAD_DIM).transpose(0, 2, 1, 3).reshape(
            DB, 2 * da_h, DA_HEAD_DIM)
        v_new = tail3[:, 2 * da_w:3 * da_w].reshape(DB, da_h, 2 * DA_HEAD_DIM)
        o_da_sq, k_new = _diff_attn_sample(page_table, to_sh(tail3[:, :da_w]), to_sh(tail3[:, da_w:2 * da_w]),
                                           v_new, cos_s, sin_s, da_q_norm[l], da_k_norm[l], da_sub_norm[l],
                                           da_lambda[l], cache_k, cache_v, layer=l, lam_init=lam_init)
        o_da_s = _pad_tail(o_da_sq.reshape(DB, da_w))
        kp.append(k_rows_p.reshape(B, L, da_h, 2 * DA_HEAD_DIM))
        vp.append(v_rows_p.reshape(B, L, da_h, 2 * DA_HEAD_DIM))
        ksm.append(k_new.reshape(DB, 2, da_h, DA_HEAD_DIM).transpose(0, 2, 1, 3).reshape(
            DB, 1, da_h, 2 * DA_HEAD_DIM))
        vsm.append(v_new.reshape(DB, 1, da_h, 2 * DA_HEAD_DIM))

        tnd = _pick_tile(D, 512)
        merged, merged_s = _matmul_merge(o_dn, o_da, o_dn_s, o_da_s, w_branch_a, w_branch_b, l,
                                         proj3, proj3_s, gate_col0=3 * da_w, tn=tnd, tm=tm)
        x, xs = _matmul(merged, merged_s, w_out, l, n_cols=D, tn=tnd, tm=tm, out_dtype=F32,
                        resid=(x, xs), name="mm_out")
        hn, hns = _rmsnorm(x, xs, ffn_norm[l])
        act, act_s = _matmul_swiglu(hn, hns, w_gate_up, l, d_ff=d_ff, tn=_pick_tile(d_ff, 256), tm=tm)
        x, xs = _matmul(act, act_s, w_down_bf, l, n_cols=D, tn=tnd,
                        tm=_pick_tile(m, 512, unit=TAIL), out_dtype=F32, resid=(x, xs), name="mm_down")

    y_prompt = x.reshape(B, L, D)
    y_sample = xs[:DB].reshape(DB, 1, D)
    return (y_prompt, y_sample, jnp.stack(kp), jnp.stack(vp), jnp.stack(ksm), jnp.stack(vsm),
            jnp.stack(sp), jnp.stack(ssm), jnp.stack(cp), jnp.stack(csm))
```

```python
import functools
import math

import jax
import jax.numpy as jnp
from jax import lax
from jax.experimental import pallas as pl
from jax.experimental.pallas import tpu as pltpu

F32 = jnp.float32
BF16 = jnp.bfloat16

RMS_EPS = 1e-6
ROPE_THETA = 10000.0
CONV_WIDTH = 4
DN_HEAD_DIM = 128
DN_CHUNK = 64
DA_HEAD_DIM = 128
LANES = 128
SUBLANES = 8
TAIL = 16
CONV_PAD = 8
VMEM_LIMIT = 56 * 1024 * 1024
DN_HEADS_PER_STEP = 2
PAGES_PER_STEP = 8


def _pick_tile(n, cap, unit=LANES):
    best = None
    t = unit
    while t <= min(n, cap):
        if n % t == 0:
            best = t
        t += unit
    assert best is not None, (n, cap)
    return best


def _nt_dot(a, b):
    return lax.dot_general(a, b, (((1,), (1,)), ((), ())), preferred_element_type=F32)


def _dot(a, b):
    return jnp.dot(a, b, preferred_element_type=F32)


def _silu(x):
    return x * jax.nn.sigmoid(x)


def _split3(x):
    hi = x.astype(BF16)
    r = x - hi.astype(F32)
    mid = r.astype(BF16)
    return hi, mid, (r - mid.astype(F32)).astype(BF16)


def _params(sem):
    return pltpu.CompilerParams(dimension_semantics=sem, vmem_limit_bytes=VMEM_LIMIT)


def _rmsnorm_kernel(x_ref, xs_ref, g_ref, o_ref, os_ref):
    def norm(x):
        ms = jnp.mean(x * x, axis=-1, keepdims=True)
        return (x * lax.rsqrt(ms + RMS_EPS) * g_ref[...]).astype(BF16)

    o_ref[...] = norm(x_ref[...])

    @pl.when(pl.program_id(0) == 0)
    def _():
        os_ref[...] = norm(xs_ref[...])


def _rmsnorm(x, xs, gain, tr=256):
    m, d = x.shape
    return pl.pallas_call(
        _rmsnorm_kernel,
        grid=(m // tr,),
        in_specs=[pl.BlockSpec((tr, d), lambda i: (i, 0)),
                  pl.BlockSpec((TAIL, d), lambda i: (0, 0)),
                  pl.BlockSpec((1, d), lambda i: (0, 0))],
        out_specs=[pl.BlockSpec((tr, d), lambda i: (i, 0)),
                   pl.BlockSpec((TAIL, d), lambda i: (0, 0))],
        out_shape=[jax.ShapeDtypeStruct((m, d), BF16), jax.ShapeDtypeStruct((TAIL, d), BF16)],
        compiler_params=_params(("arbitrary",)),
        name="rmsnorm",
    )(x, xs, gain.reshape(1, d))


CAST_ROWS = 256
CAST_COLS = 512


def _cast_weight(b_ref, bscr, b_next=None, shift=0):
    k, tn = bscr.shape
    step = CAST_ROWS if k % CAST_ROWS == 0 else k

    def body(c, carry):
        r0 = pl.multiple_of(c * step, step)
        w = b_ref[pl.ds(r0, step), :]
        if b_next is not None:
            w = jnp.concatenate([w, b_next[pl.ds(r0, step), :]], axis=1)[:, shift:shift + tn]
        bscr[pl.ds(r0, step), :] = w.astype(BF16)
        return carry

    lax.fori_loop(0, k // step, body, 0)


def _cast_weight_t(bt_ref, bscr, bt_next=None, shift=0):
    tn, k = bscr.shape
    step = CAST_COLS if k % CAST_COLS == 0 else k
    for c0 in range(0, k, step):
        cols = slice(c0, c0 + step)
        bscr[0:tn - shift, cols] = bt_ref[shift:tn, cols].astype(BF16)
        if shift:
            bscr[tn - shift:tn, cols] = bt_next[:, cols].astype(BF16)


def _mm_plain_kernel(*refs, shift, cast, resid, transposed):
    refs = list(refs)
    a_ref, as_ref, b_ref = refs[:3]
    del refs[:3]
    bn_ref = refs.pop(0) if shift else None
    r_ref, rs_ref = (refs.pop(0), refs.pop(0)) if resid else (None, None)
    o_ref, os_ref = refs[:2]
    w = refs[2] if cast else b_ref
    first = pl.program_id(1) == 0

    if cast:
        @pl.when(first)
        def _():
            (_cast_weight_t if transposed else _cast_weight)(b_ref, w, bn_ref, shift)

    def out(a, r):
        y = _nt_dot(a[...], w[...]) if transposed else _dot(a[...], w[...])
        return y if r is None else r[...] + y

    o_ref[...] = out(a_ref, r_ref).astype(o_ref.dtype)

    @pl.when(first)
    def _():
        os_ref[...] = out(as_ref, rs_ref).astype(os_ref.dtype)


def _matmul(a, a_s, b, layer, *, n_cols, tn, tm, out_dtype, b_col0=0, shift=0, resid=None, transposed=False,
            name="mm"):
    m, k = a.shape
    assert m % tm == 0 and b_col0 % tn == 0 and n_cols % tn == 0 and 0 <= shift < LANES
    joff = b_col0 // tn
    cast = b.dtype != BF16
    assert cast or not shift
    in_specs = [pl.BlockSpec((tm, k), lambda j, i: (i, 0)),
                pl.BlockSpec((TAIL, k), lambda j, i: (0, 0))]
    args = [a, a_s, b]
    if transposed:
        assert cast and shift % SUBLANES == 0 and (not shift or (tn % shift == 0 and b_col0 % shift == 0))
        in_specs.append(pl.BlockSpec((None, tn, k), lambda j, i: (layer, j + joff, 0)))
        if shift:
            per = tn // shift
            in_specs.append(pl.BlockSpec((None, shift, k), lambda j, i: (layer, (j + joff + 1) * per, 0)))
            args.append(b)
    else:
        in_specs.append(pl.BlockSpec((None, k, tn), lambda j, i: (layer, 0, j + joff)))
        if shift:
            per = tn // LANES
            in_specs.append(pl.BlockSpec((None, k, LANES), lambda j, i: (layer, 0, (j + joff + 1) * per)))
            args.append(b)
    if resid is not None:
        in_specs += [pl.BlockSpec((tm, tn), lambda j, i: (i, j)),
                     pl.BlockSpec((TAIL, tn), lambda j, i: (0, j))]
        args += list(resid)
    return pl.pallas_call(
        functools.partial(_mm_plain_kernel, shift=shift, cast=cast, resid=resid is not None,
                          transposed=transposed),
        grid=(n_cols // tn, m // tm),
        in_specs=in_specs,
        out_specs=[pl.BlockSpec((tm, tn), lambda j, i: (i, j)),
                   pl.BlockSpec((TAIL, tn), lambda j, i: (0, j))],
        out_shape=[jax.ShapeDtypeStruct((m, n_cols), out_dtype),
                   jax.ShapeDtypeStruct((TAIL, n_cols), out_dtype)],
        scratch_shapes=[pltpu.VMEM((tn, k) if transposed else (k, tn), BF16)] if cast else [],
        compiler_params=_params(("arbitrary", "arbitrary")),
        name=name,
    )(*args)


def _mm_swiglu_kernel(a_ref, as_ref, bg_ref, bu_ref, o_ref, os_ref, sg, su):
    first = pl.program_id(1) == 0

    @pl.when(first)
    def _():
        _cast_weight(bg_ref, sg)
        _cast_weight(bu_ref, su)

    def out(a):
        return (_silu(_dot(a, sg[...])) * _dot(a, su[...])).astype(BF16)

    o_ref[...] = out(a_ref[...])

    @pl.when(first)
    def _():
        os_ref[...] = out(as_ref[...])


def _matmul_swiglu(a, a_s, w_gate_up, layer, *, d_ff, tn, tm):
    m, k = a.shape
    uoff = d_ff // tn
    return pl.pallas_call(
        _mm_swiglu_kernel,
        grid=(d_ff // tn, m // tm),
        in_specs=[pl.BlockSpec((tm, k), lambda j, i: (i, 0)),
                  pl.BlockSpec((TAIL, k), lambda j, i: (0, 0)),
                  pl.BlockSpec((None, k, tn), lambda j, i: (layer, 0, j)),
                  pl.BlockSpec((None, k, tn), lambda j, i: (layer, 0, j + uoff))],
        out_specs=[pl.BlockSpec((tm, tn), lambda j, i: (i, j)),
                   pl.BlockSpec((TAIL, tn), lambda j, i: (0, j))],
        out_shape=[jax.ShapeDtypeStruct((m, d_ff), BF16), jax.ShapeDtypeStruct((TAIL, d_ff), BF16)],
        scratch_shapes=[pltpu.VMEM((k, tn), BF16), pltpu.VMEM((k, tn), BF16)],
        compiler_params=_params(("arbitrary", "arbitrary")),
        name="mm_swiglu",
    )(a, a_s, w_gate_up, w_gate_up)


def _mm_merge_kernel(a1_ref, a2_ref, a1s_ref, a2s_ref, b1_ref, b2_ref, g1_ref, g2_ref, g1s_ref, g2s_ref,
                     o_ref, os_ref, s1, s2):
    first = pl.program_id(1) == 0

    @pl.when(first)
    def _():
        _cast_weight(b1_ref, s1)
        _cast_weight(b2_ref, s2)

    def out(a1, a2, g1, g2):
        return (jax.nn.sigmoid(g1[...]) * _dot(a1[...], s1[...])
                + jax.nn.sigmoid(g2[...]) * _dot(a2[...], s2[...])).astype(BF16)

    o_ref[...] = out(a1_ref, a2_ref, g1_ref, g2_ref)

    @pl.when(first)
    def _():
        os_ref[...] = out(a1s_ref, a2s_ref, g1s_ref, g2s_ref)


def _matmul_merge(a1, a2, a1s, a2s, b1, b2, layer, gates, gates_s, *, gate_col0, tn, tm):
    m, k = a1.shape
    n = b1.shape[2]
    assert gate_col0 % tn == 0
    g1off = gate_col0 // tn
    g2off = g1off + n // tn
    row = lambda blk: pl.BlockSpec((tm, blk), lambda j, i: (i, 0))
    tail = lambda blk: pl.BlockSpec((TAIL, blk), lambda j, i: (0, 0))
    wt = pl.BlockSpec((None, k, tn), lambda j, i: (layer, 0, j))
    return pl.pallas_call(
        _mm_merge_kernel,
        grid=(n // tn, m // tm),
        in_specs=[row(k), row(k), tail(k), tail(k), wt, wt,
                  pl.BlockSpec((tm, tn), lambda j, i: (i, j + g1off)),
                  pl.BlockSpec((tm, tn), lambda j, i: (i, j + g2off)),
                  pl.BlockSpec((TAIL, tn), lambda j, i: (0, j + g1off)),
                  pl.BlockSpec((TAIL, tn), lambda j, i: (0, j + g2off))],
        out_specs=[pl.BlockSpec((tm, tn), lambda j, i: (i, j)),
                   pl.BlockSpec((TAIL, tn), lambda j, i: (0, j))],
        out_shape=[jax.ShapeDtypeStruct((m, n), BF16), jax.ShapeDtypeStruct((TAIL, n), BF16)],
        scratch_shapes=[pltpu.VMEM((k, tn), BF16), pltpu.VMEM((k, tn), BF16)],
        compiler_params=_params(("arbitrary", "arbitrary")),
        name="mm_merge",
    )(a1, a2, a1s, a2s, b1, b2, gates, gates, gates_s, gates_s)


DN_SUPER = 4
DN_INTERLEAVE = 2


def _delta_kernel(hpar_ref,
                  qp_ref, kp_ref, vp_ref, z_ref, ba_ref,
                  cbq_ref, cbk_ref, cbv_ref, cwq_ref, cwk_ref, cwv_ref,
                  s0_ref, gn_ref,
                  o_ref, sn_ref,
                  xq_s, xk_s, xv_s, aq_s, n_s, oc_s, gt_s, mf_s, mb_s,
                  *, seq, valid, n_heads, single):
    C = DN_CHUNK
    D = DN_HEAD_DIM
    P = DN_HEADS_PER_STEP
    hp = pl.program_id(1)
    n_chunks = seq // C
    U = min(DN_SUPER, n_chunks)
    R = U * C
    AQ = D + C

    def rows(ref, r0, n):
        if not single:
            return ref[pl.ds(r0, n), :]
        assert n == seq
        first = lax.broadcasted_iota(jnp.int32, (seq, 1), 0) == 0
        return jnp.where(first, ref[pl.ds(pl.program_id(0), 1), :], 0.0)

    for xs, cb, xp in ((xq_s, cbq_ref, qp_ref), (xk_s, cbk_ref, kp_ref), (xv_s, cbv_ref, vp_ref)):
        xs[0:CONV_PAD, :] = cb[...]
        xs[CONV_PAD:, :] = rows(xp, 0, seq)

    row = lax.broadcasted_iota(jnp.int32, (R, R), 0)
    col = lax.broadcasted_iota(jnp.int32, (R, R), 1)
    same = (row // C) == (col // C)
    mf_s[0] = (same & (row >= col)).astype(F32)
    mf_s[1] = (same & (row > col)).astype(F32)
    mf_s[2] = (row == col).astype(F32)
    mb_s[0] = (same & (row >= col)).astype(BF16)
    mb_s[1] = (same & (row <= col)).astype(BF16)
    mb_s[2] = same.astype(BF16)
    ones_bf = jnp.ones((R, R), BF16)
    lane = lax.broadcasted_iota(jnp.int32, (R, LANES), 1)

    def conv_silu(xs, cw_ref, r0):
        xc = xs[pl.ds(r0, R + CONV_PAD), :]
        w = cw_ref[...]
        first = CONV_PAD - (CONV_WIDTH - 1)
        y = xc[first:first + R] * w[0:1]
        for t in range(1, CONV_WIDTH):
            y = y + xc[first + t:first + t + R] * w[t:t + 1]
        return _silu(y)

    def l2n(x):
        return x * lax.rsqrt(jnp.sum(x * x, axis=-1, keepdims=True) + RMS_EPS)

    def exact_dot(sel_bf, terms):
        return sum(_dot(sel_bf, t) for t in terms)

    def each(fn, *lists):
        return [fn(*args) for args in zip(*lists)]

    S = DN_INTERLEAVE if (n_chunks // U) % DN_INTERLEAVE == 0 else 1
    units = [(s, p) for s in range(S) for p in range(P)]

    def prep(it, carry):
        r0s = [pl.multiple_of((it * S + s) * R, R) for s in range(S)]
        qc = [conv_silu(xq_s, cwq_ref, r0) for r0 in r0s]
        kc = [conv_silu(xk_s, cwk_ref, r0) for r0 in r0s]
        vc = [conv_silu(xv_s, cwv_ref, r0) for r0 in r0s]
        ba = [rows(ba_ref, r0, R) for r0 in r0s]
        q = [l2n(qc[s][:, p * D:(p + 1) * D]) * (D ** -0.5) for s, p in units]
        k = [l2n(kc[s][:, p * D:(p + 1) * D]) for s, p in units]
        v = [vc[s][:, p * D:(p + 1) * D] for s, p in units]
        neg_a = -jnp.exp(hpar_ref[0:1, :])
        beta_all = [jax.nn.sigmoid(t) for t in ba]
        xs_all = [t + hpar_ref[1:2, :] for t in ba]
        g_all = [neg_a * (jnp.maximum(x, 0.0) + jnp.log1p(jnp.exp(-jnp.abs(x)))) for x in xs_all]
        beta, g = [], []
        for s, p in units:
            h = hp * P + p
            beta.append(jnp.sum(jnp.where(lane == h, beta_all[s], 0.0), axis=-1, keepdims=True))
            g.append(jnp.sum(jnp.where(lane == n_heads + h, g_all[s], 0.0), axis=-1, keepdims=True))
        if valid < seq:
            live = [(r0s[s] + lax.broadcasted_iota(jnp.int32, (R, 1), 0)) < valid for s, _ in units]
            dead = lambda t, keep: jnp.where(keep, t, 0.0)
            k, v, beta, g = each(dead, k, live), each(dead, v, live), each(dead, beta, live), each(dead, g, live)

        widen = lambda t: jnp.concatenate([t] * (R // D), axis=1) if R > D else t[:, :R]
        g3 = each(lambda t: _split3(jnp.broadcast_to(t, (R, D))), g)
        gsum = each(lambda t3: exact_dot(mb_s[0], t3), g3)
        g_last = each(lambda t3: exact_dot(mb_s[2], t3), g3)
        gj = each(lambda t3: exact_dot(ones_bf, [widen(t) * mb_s[1] for t in t3]), g3)
        decay = each(lambda a, b: jnp.exp(jnp.where(mf_s[0] > 0.5, widen(a) - b, -jnp.inf)), gsum, gj)
        kbf = each(lambda t: t.astype(BF16), k)
        m = each(lambda kb, d, b: _nt_dot(kb, kb) * d * b * mf_s[1], kbf, decay, beta)
        x_pow = each(lambda t: -t, m)
        t_inv = each(lambda t: mf_s[2] + t, x_pow)
        for _ in range(int(math.log2(C)) - 1):
            xb = each(lambda t: t.astype(BF16), x_pow)
            x_pow = each(lambda t: _dot(t, t), xb)
            t_inv = each(lambda t, xp: t + _dot(t.astype(BF16), xp.astype(BF16)), t_inv, x_pow)
        m_hi = each(lambda t: t.astype(BF16), m)
        m_lo = each(lambda t, hi: (t - hi.astype(F32)).astype(BF16), m, m_hi)
        t_hi = each(lambda t: t.astype(BF16), t_inv)
        t_lo = each(lambda t, hi: (t - hi.astype(F32)).astype(BF16), t_inv, t_hi)
        resid = each(lambda t, mh, ml, th, tl: mf_s[2] - t - (_dot(mh, th) + (_dot(mh, tl) + _dot(ml, th))),
                     t_inv, m_hi, m_lo, t_hi, t_lo)
        t_inv = each(lambda t, th, r: t + _dot(th, r.astype(BF16)), t_inv, t_hi, resid)

        e_g = each(jnp.exp, gsum)
        wu = each(lambda t, kk, vv, b, e: _dot(t.astype(BF16), jnp.concatenate(
            [kk * (b * e), vv * b], axis=1).astype(BF16)).astype(BF16), t_inv, k, v, beta, e_g)
        qk = each(lambda qq, kb, d: (_nt_dot(qq.astype(BF16), kb) * d).astype(BF16), q, kbf, decay)
        qwo = each(_dot, qk, wu)
        k_dec = each(lambda kk, gl, gs: (kk * jnp.exp(gl - gs)).astype(BF16), k, g_last, gsum)
        g_tot = each(jnp.exp, g_last)
        q_eff = each(lambda qq, e, t: (qq * e - t[:, :D]).astype(BF16), q, e_g, qwo)
        for i, (s, p) in enumerate(units):
            oc_s[p, pl.ds(r0s[s], R), :] = qwo[i][:, D:]
        for u in range(U):
            rws = slice(u * C, (u + 1) * C)
            an = each(lambda kd, w: lax.dot_general(kd[rws], w[rws], (((0,), (0,)), ((), ())),
                                                    preferred_element_type=F32), k_dec, wu)
            for i, (s, p) in enumerate(units):
                c = (it * S + s) * U + u
                base = pl.multiple_of(c * (P * AQ) + p * AQ, SUBLANES)
                aq_s[pl.ds(base, D), :] = an[i][:, :D].astype(BF16)
                aq_s[pl.ds(base + D, C), :] = q_eff[i][rws]
                n_s[p, pl.ds(pl.multiple_of(c * D, D), D), :] = an[i][:, D:]
                gt_s[p, pl.ds(pl.multiple_of(c * SUBLANES, SUBLANES), SUBLANES), :] = (
                    g_tot[i][u * C:u * C + SUBLANES])
        return carry

    lax.fori_loop(0, n_chunks // (U * S), prep, 0)

    gain = gn_ref[...]

    def scan(c, s):
        r0 = pl.multiple_of(c * C, C)
        z = rows(z_ref, r0, C)
        r = _dot(aq_s[pl.ds(pl.multiple_of(c * (P * AQ), SUBLANES), P * AQ), :], s.astype(BF16))
        new_s = []
        outs = []
        for p in range(P):
            rp = r[p * AQ:(p + 1) * AQ, p * D:(p + 1) * D]
            g_tot = gt_s[p, pl.ds(pl.multiple_of(c * SUBLANES, SUBLANES), 1), :]
            new_s.append(s[:, p * D:(p + 1) * D] * g_tot - rp[:D]
                         + n_s[p, pl.ds(pl.multiple_of(c * D, D), D), :])
            o = rp[D:] + oc_s[p, pl.ds(r0, C), :]
            ms = jnp.mean(o * o, axis=-1, keepdims=True)
            y = o * lax.rsqrt(ms + RMS_EPS) * gain
            outs.append(y * _silu(z[:, p * D:(p + 1) * D]))
        o_ref[pl.ds(r0, C), :] = jnp.concatenate(outs, axis=1).astype(o_ref.dtype)
        return jnp.concatenate(new_s, axis=1)

    final = lax.fori_loop(0, n_chunks, scan, jnp.concatenate([s0_ref[p] for p in range(P)], axis=1))
    for p in range(P):
        sn_ref[p] = final[:, p * D:(p + 1) * D]


def _delta_branch(proj1, ba, conv_buf8, conv_w, s0, a_log, dt_bias, out_gain, *, batch, seq, valid, n_heads,
                  single=False):
    D = DN_HEAD_DIM
    C = DN_CHUNK
    H = n_heads
    P = DN_HEADS_PER_STEP
    G = H // P
    PD = P * D
    assert H % P == 0
    n_chunks = seq // C
    assert n_chunks % min(DN_SUPER, n_chunks) == 0
    rr = min(DN_SUPER, n_chunks) * C
    if single:
        assert valid == 1 and seq == C
        blk = lambda off: pl.BlockSpec((TAIL, PD), lambda b, h: (0, off + h))
        ba_spec = pl.BlockSpec((TAIL, LANES), lambda b, h: (0, 0))
    else:
        blk = lambda off: pl.BlockSpec((seq, PD), lambda b, h: (b, off + h))
        ba_spec = pl.BlockSpec((seq, LANES), lambda b, h: (b, 0))
    cbs = lambda off: pl.BlockSpec((None, CONV_PAD, PD), lambda b, h: (b, 0, off + h))
    cws = lambda off: pl.BlockSpec((CONV_WIDTH, PD), lambda b, h: (0, off + h))
    hpar = jnp.zeros((SUBLANES, LANES), F32).at[0, H:2 * H].set(a_log).at[1, H:2 * H].set(dt_bias)
    o, s_new = pl.pallas_call(
        functools.partial(_delta_kernel, seq=seq, valid=valid, n_heads=H, single=single),
        grid=(batch, G),
        in_specs=[pl.BlockSpec((SUBLANES, LANES), lambda b, h: (0, 0)),
                  blk(0), blk(G), blk(2 * G), blk(3 * G), ba_spec,
                  cbs(0), cbs(G), cbs(2 * G), cws(0), cws(G), cws(2 * G),
                  pl.BlockSpec((None, P, D, D), lambda b, h: (b, h, 0, 0)),
                  pl.BlockSpec((1, D), lambda b, h: (0, 0))],
        out_specs=[pl.BlockSpec((seq, PD), lambda b, h: (b, h)),
                   pl.BlockSpec((None, P, D, D), lambda b, h: (b, h, 0, 0))],
        out_shape=[jax.ShapeDtypeStruct((batch * seq, H * D), BF16),
                   jax.ShapeDtypeStruct((batch, H, D, D), F32)],
        scratch_shapes=[pltpu.VMEM((seq + CONV_PAD, PD), F32)] * 3 + [
            pltpu.VMEM((n_chunks * P * (D + C), D), BF16),
            pltpu.VMEM((P, n_chunks * D, D), F32),
            pltpu.VMEM((P, seq, D), F32),
            pltpu.VMEM((P, n_chunks * SUBLANES, LANES), F32),
            pltpu.VMEM((3, rr, rr), F32), pltpu.VMEM((3, rr, rr), BF16)],
        compiler_params=_params(("arbitrary", "arbitrary")),
        name="delta",
    )(hpar, proj1, proj1, proj1, proj1, ba,
      conv_buf8, conv_buf8, conv_buf8, conv_w, conv_w, conv_w, s0, out_gain.reshape(1, D))
    return o, s_new


def _norm_rope(x, gain, cos, sin):
    ms = jnp.mean(x * x, axis=-1, keepdims=True)
    y = x * lax.rsqrt(ms + RMS_EPS) * gain
    return y * cos + pltpu.roll(y, DA_HEAD_DIM // 2, 1) * sin


def _lambda(lp, lam_init):
    return (jnp.exp(jnp.sum(lp[0:1] * lp[1:2], axis=1, keepdims=True))
            - jnp.exp(jnp.sum(lp[2:3] * lp[3:4], axis=1, keepdims=True)) + lam_init)


def _da_kernel(q_ref, k_ref, v_ref, cos_ref, sin_ref, qg_ref, kg_ref, sg_ref, lp_ref, *refs,
               seq, tq, tk, lam_init):
    o_ref, krow_ref, vrow_ref, kbf_s, vbf_s = refs[-5:]
    dh = DA_HEAD_DIM
    qi = pl.program_id(2)
    per = tq // tk
    scale = dh ** -0.5

    @pl.when(qi == 0)
    def _():
        kg = kg_ref[...]

        def body(c, carry):
            r0 = pl.multiple_of(c * tk, tk)
            kb = k_ref[pl.ds(r0, tk), :]
            cos = cos_ref[pl.ds(r0, tk), :]
            sin = sin_ref[pl.ds(r0, tk), :]
            kr = jnp.concatenate([_norm_rope(kb[:, :dh], kg, cos, sin),
                                  _norm_rope(kb[:, dh:], kg, cos, sin)], axis=1)
            krow_ref[pl.ds(r0, tk), :] = kr
            kbf_s[pl.ds(r0, tk), :] = kr.astype(BF16)
            vb = v_ref[pl.ds(r0, tk), :]
            vrow_ref[pl.ds(r0, tk), :] = vb
            vbf_s[pl.ds(r0, tk), :] = vb.astype(BF16)
            return carry

        lax.fori_loop(0, seq // tk, body, 0)

    q0 = pl.multiple_of(qi * tq, tq)
    qb = q_ref[...]
    cos = cos_ref[pl.ds(q0, tq), :]
    sin = sin_ref[pl.ds(q0, tq), :]
    qg = qg_ref[...]
    q1 = _norm_rope(qb[:, :dh], qg, cos, sin).astype(BF16)
    q2 = _norm_rope(qb[:, dh:], qg, cos, sin).astype(BF16)
    lam = _lambda(lp_ref[...], lam_init)
    q_pos = q0 + lax.broadcasted_iota(jnp.int32, (tq, tk), 0)
    k_off = lax.broadcasted_iota(jnp.int32, (tq, tk), 1)

    def scores(kt, masked):
        kb = kbf_s[pl.ds(pl.multiple_of(kt * tk, tk), tk), :]
        s1 = _nt_dot(q1, kb[:, :dh]) * scale
        s2 = _nt_dot(q2, kb[:, dh:]) * scale
        if masked:
            visible = kt * tk + k_off <= q_pos
            s1 = jnp.where(visible, s1, -jnp.inf)
            s2 = jnp.where(visible, s2, -jnp.inf)
        return s1, s2

    def fold(x, op):
        y = x[:, :LANES]
        for c in range(1, tk // LANES):
            y = op(y, x[:, c * LANES:(c + 1) * LANES])
        return y

    def over_tiles(step, carry):
        def group(j, c):
            for u in range(per):
                c = step(j * per + u, c, False)
            return c

        carry = lax.fori_loop(0, qi, group, carry)
        for u in range(per):
            carry = step(qi * per + u, carry, True)
        return carry

    def max_step(kt, carry, masked):
        s1, s2 = scores(kt, masked)
        return jnp.maximum(carry[0], fold(s1, jnp.maximum)), jnp.maximum(carry[1], fold(s2, jnp.maximum))

    neg = jnp.full((tq, LANES), -jnp.inf, F32)
    mx = over_tiles(max_step, (neg, neg))
    m1 = jnp.max(mx[0], axis=-1, keepdims=True)
    m2 = jnp.max(mx[1], axis=-1, keepdims=True)

    def sum_step(kt, carry, masked):
        l1, l2, a1, a2 = carry
        s1, s2 = scores(kt, masked)
        e1 = jnp.exp(s1 - m1)
        e2 = jnp.exp(s2 - m2)
        vb = vbf_s[pl.ds(pl.multiple_of(kt * tk, tk), tk), :]
        return (l1 + fold(e1, jnp.add), l2 + fold(e2, jnp.add),
                a1 + _dot(e1.astype(BF16), vb), a2 + _dot(e2.astype(BF16), vb))

    zl = jnp.zeros((tq, LANES), F32)
    za = jnp.zeros((tq, 2 * dh), F32)
    l1, l2, a1, a2 = over_tiles(sum_step, (zl, zl, za, za))
    inv1 = 1.0 / jnp.sum(l1, axis=-1, keepdims=True)
    inv2 = 1.0 / jnp.sum(l2, axis=-1, keepdims=True)
    o = a1 * inv1 - lam * (a2 * inv2)
    ms = jnp.mean(o * o, axis=-1, keepdims=True)
    o_ref[...] = (o * lax.rsqrt(ms + RMS_EPS) * sg_ref[...] * (1.0 - lam_init)).astype(o_ref.dtype)


def _diff_attn_prompt(proj3, cos, sin, q_gain, k_gain, sub_gain, lam_p, kv_rows, *, layer, depth, batch, seq,
                      n_heads, lam_init, tq, tk):
    dh = DA_HEAD_DIM
    H = n_heads
    nq = seq // tq
    full = lambda shape: pl.BlockSpec(shape, lambda b, h, qi: (0, 0))
    n_in = 9
    carried = [] if kv_rows is None else list(kv_rows)
    rows_spec = pl.BlockSpec((None, seq, 2 * dh), lambda b, h, qi: (layer, b, h))
    rows_shape = jax.ShapeDtypeStruct((depth, batch * seq, H * 2 * dh), F32)
    return pl.pallas_call(
        functools.partial(_da_kernel, seq=seq, tq=tq, tk=tk, lam_init=lam_init),
        grid=(batch, H, nq),
        in_specs=[pl.BlockSpec((tq, 2 * dh), lambda b, h, qi: (b * nq + qi, h)),
                  pl.BlockSpec((seq, 2 * dh), lambda b, h, qi: (b, H + h)),
                  pl.BlockSpec((seq, 2 * dh), lambda b, h, qi: (b, 2 * H + h)),
                  full((seq, dh)), full((seq, dh)), full((1, dh)), full((1, dh)),
                  full((1, 2 * dh)), full((4, dh))] + [pl.BlockSpec(memory_space=pl.ANY)] * len(carried),
        out_specs=[pl.BlockSpec((tq, 2 * dh), lambda b, h, qi: (b * nq + qi, h)), rows_spec, rows_spec],
        out_shape=[jax.ShapeDtypeStruct((batch * seq, H * 2 * dh), BF16), rows_shape, rows_shape],
        input_output_aliases={n_in + i: 1 + i for i in range(len(carried))},
        scratch_shapes=[pltpu.VMEM((seq, 2 * dh), BF16), pltpu.VMEM((seq, 2 * dh), BF16)],
        compiler_params=_params(("arbitrary", "arbitrary", "arbitrary")),
        name="diff_attn_prompt",
    )(proj3, proj3, proj3, cos, sin, q_gain.reshape(1, dh), k_gain.reshape(1, dh),
      sub_gain.reshape(1, 2 * dh), lam_p, *carried)


def _decode_kernel(pt_ref, q_ref, kn_ref, vn_ref, cos_ref, sin_ref, qg_ref, kg_ref, sg_ref, lp_ref,
                   *refs, n_steps, pages_per_step, n_heads, lam_init):
    G = pages_per_step
    kc_refs, vc_refs = refs[:G], refs[G:2 * G]
    o_ref, kout_ref, qmat_s, sc_s, a_s, acc_s, new_s = refs[2 * G:]
    dh = DA_HEAD_DIM
    H = n_heads
    R = 2 * H
    page = kc_refs[0].shape[0]
    PH = page * H
    p = pl.program_id(1)
    scale = dh ** -0.5
    own_head = (lax.broadcasted_iota(jnp.int32, (R, PH), 1) % H
                == lax.broadcasted_iota(jnp.int32, (R, PH), 0) % H)

    @pl.when(p == 0)
    def _():
        cos = cos_ref[...]
        sin = sin_ref[...]
        qn = _norm_rope(q_ref[...], qg_ref[...], cos, sin)
        kn = _norm_rope(kn_ref[...], kg_ref[...], cos, sin)
        kout_ref[...] = kn
        sub = lax.broadcasted_iota(jnp.int32, (R, 2 * dh), 0) // H
        half = lax.broadcasted_iota(jnp.int32, (R, 2 * dh), 1) // dh
        qmat_s[...] = jnp.where(sub == half, jnp.concatenate([qn, qn], axis=1), 0.0).astype(BF16)
        s_new = jnp.sum(qn.astype(BF16).astype(F32) * kn.astype(BF16).astype(F32),
                        axis=-1, keepdims=True) * scale
        new_s[...] = jnp.broadcast_to(s_new, (R, LANES))
        acc_s[...] = jnp.zeros_like(acc_s)

    @pl.when(p < n_steps)
    def _():
        for g in range(G):
            kp = kc_refs[g][...].reshape(PH, 2 * dh).astype(BF16)
            s = _nt_dot(qmat_s[...], kp) * scale
            sc_s[p * G + g] = jnp.where(own_head, s, -jnp.inf)

    @pl.when(p == n_steps - 1)
    def _():
        n_pages = n_steps * G
        s_new = new_s[...]
        unroll = 8 if n_pages % 8 == 0 else 1

        def pages(fn, init):
            def group(j, acc):
                for u in range(unroll):
                    acc = fn(j * unroll + u, acc)
                return acc
            return lax.fori_loop(0, n_pages // unroll, group, init)

        m_el = pages(lambda i, acc: jnp.maximum(acc, sc_s[i]), jnp.full((R, PH), -jnp.inf, F32))
        m = jnp.maximum(jnp.max(m_el, axis=-1, keepdims=True), s_new[:, 0:1])
        l_el = pages(lambda i, acc: acc + jnp.exp(sc_s[i] - m), jnp.zeros((R, PH), F32))
        e_new = jnp.exp(s_new - m)
        l = jnp.sum(l_el, axis=-1, keepdims=True) + e_new[:, 0:1]
        inv = 1.0 / l
        lam = _lambda(lp_ref[...], lam_init)

        def weights(i, carry):
            pr = jnp.exp(sc_s[i] - m) * inv
            a = pr[:H] - lam * pr[H:]
            a_s[i] = jnp.concatenate([a, jnp.zeros_like(a)], axis=0).astype(BF16)
            return carry

        pages(weights, 0)
        pn = e_new * inv
        new_s[...] = jnp.concatenate([pn[:H] - lam * pn[H:], jnp.zeros((R - H, LANES), F32)], axis=0)

    @pl.when(p >= n_steps)
    def _():
        acc = acc_s[...]
        for g in range(G):
            vp = vc_refs[g][...].reshape(PH, 2 * dh).astype(BF16)
            acc = acc + _dot(a_s[(p - n_steps) * G + g], vp)
        acc_s[...] = acc

    @pl.when(p == 2 * n_steps - 1)
    def _():
        a_new = new_s[...][:H, 0:1].astype(BF16).astype(F32)
        o = acc_s[...][:H] + a_new * vn_ref[...].astype(BF16).astype(F32)
        ms = jnp.mean(o * o, axis=-1, keepdims=True)
        o_ref[...] = (o * lax.rsqrt(ms + RMS_EPS) * sg_ref[...] * (1.0 - lam_init)).astype(o_ref.dtype)


def _diff_attn_sample(page_table, q_sh, k_sh, v_new, cos, sin, q_gain, k_gain, sub_gain, lam_p,
                      cache_k, cache_v, *, layer, lam_init):
    dh = DA_HEAD_DIM
    B, n_pages = page_table.shape
    G = max(g for g in range(1, PAGES_PER_STEP + 1) if n_pages % g == 0)
    H = v_new.shape[1]
    R = 2 * H
    page = cache_k.shape[2]
    n_steps = n_pages // G
    per_b = lambda shape: pl.BlockSpec((None,) + shape, lambda b, p, pt: (b, 0, 0))
    full = lambda shape: pl.BlockSpec(shape, lambda b, p, pt: (0, 0))

    def k_page(g):
        return pl.BlockSpec((None, None, page, H, 2 * dh),
                            lambda b, p, pt: (layer, pt[b, jnp.minimum(p, n_steps - 1) * G + g], 0, 0, 0))

    def v_page(g):
        return pl.BlockSpec((None, None, page, H, 2 * dh),
                            lambda b, p, pt: (layer, pt[b, jnp.maximum(p - n_steps, 0) * G + g], 0, 0, 0))

    grid_spec = pltpu.PrefetchScalarGridSpec(
        num_scalar_prefetch=1,
        grid=(B, 2 * n_steps),
        in_specs=[per_b((R, dh)), per_b((R, dh)), per_b((H, 2 * dh)),
                  full((1, dh)), full((1, dh)), full((1, dh)), full((1, dh)),
                  full((1, 2 * dh)), full((4, dh))]
                 + [k_page(g) for g in range(G)] + [v_page(g) for g in range(G)],
        out_specs=[per_b((H, 2 * dh)), per_b((R, dh))],
        scratch_shapes=[pltpu.VMEM((R, 2 * dh), BF16),
                        pltpu.VMEM((n_pages, R, page * H), F32),
                        pltpu.VMEM((n_pages, R, page * H), BF16),
                        pltpu.VMEM((R, 2 * dh), F32),
                        pltpu.VMEM((R, LANES), F32)])
    o, k_out = pl.pallas_call(
        functools.partial(_decode_kernel, n_steps=n_steps, pages_per_step=G, n_heads=H, lam_init=lam_init),
        grid_spec=grid_spec,
        out_shape=[jax.ShapeDtypeStruct((B, H, 2 * dh), BF16),
                   jax.ShapeDtypeStruct((B, R, dh), F32)],
        compiler_params=_params(("arbitrary", "arbitrary")),
        name="diff_attn_sample",
    )(page_table, q_sh, k_sh, v_new, cos, sin, q_gain.reshape(1, dh), k_gain.reshape(1, dh),
      sub_gain.reshape(1, 2 * dh), lam_p, *([cache_k] * G), *([cache_v] * G))
    return o, k_out


def _rope_tables(pos):
    half = DA_HEAD_DIM // 2
    inv_freq = ROPE_THETA ** (-jnp.arange(half, dtype=F32) / half)
    ang = pos.astype(F32)[:, None] * inv_freq[None, :]
    cos = jnp.cos(ang)
    sin = jnp.sin(ang)
    return jnp.concatenate([cos, cos], axis=1), jnp.concatenate([-sin, sin], axis=1)


def _pad_tail(t):
    return jnp.concatenate([t, jnp.zeros((TAIL - t.shape[0],) + t.shape[1:], t.dtype)], axis=0)


def kernel(x_prompt, x_sample, cache_k, cache_v, state_delta, state_conv, page_table, attn_norm, w_in, dn_conv, dn_a_log, dn_dt_bias, dn_out_norm, da_q_norm, da_k_norm, da_lambda, da_sub_norm, w_branch_a, w_branch_b, w_out, ffn_norm, w_gate_up, w_down):
    B, L, D = x_prompt.shape
    DB = x_sample.shape[0]
    depth = w_in.shape[0]
    assert x_sample.shape[1] == 1 and DB <= TAIL and L % DN_CHUNK == 0
    n_pages = page_table.shape[1]
    page = cache_k.shape[2]
    past_len = n_pages * page
    dn_w = dn_conv.shape[2] // 3
    dn_h = dn_w // DN_HEAD_DIM
    da_h = cache_k.shape[3]
    da_w = da_h * 2 * DA_HEAD_DIM
    d_ff = w_down.shape[1]
    off_beta = 4 * dn_w
    shift = 2 * dn_h
    n3 = 3 * da_w + 2 * D
    assert shift < LANES and w_in.shape[2] == off_beta + shift + n3

    m = B * L
    tm = _pick_tile(m, 1024, unit=TAIL)
    x = x_prompt.reshape(m, D)
    xs = _pad_tail(x_sample.reshape(DB, D))
    cos_p, sin_p = _rope_tables(jnp.arange(L))
    cos_s, sin_s = _rope_tables(past_len + jnp.arange(1))
    tq = _pick_tile(L, 512)
    tk = _pick_tile(tq, 256)
    zero_buf = jnp.zeros((B, CONV_PAD, 3 * dn_w), F32)
    zero_state = jnp.zeros((B, dn_h, DN_HEAD_DIM, DN_HEAD_DIM), F32)
    seq_s = DN_CHUNK
    tn3 = _pick_tile(math.gcd(n3, off_beta), 512)
    w_down_bf = w_down.astype(BF16)
    w_in_t = jnp.swapaxes(w_in, 1, 2)

    ksm, vsm, sp, ssm, cp, csm = [], [], [], [], [], []
    kv_rows = None
    for l in range(depth):
        lam_init = 0.8 - 0.6 * math.exp(-0.3 * l)
        h, hs = _rmsnorm(x, xs, attn_norm[l])
        proj1, proj1_s = _matmul(h, hs, w_in_t, l, n_cols=off_beta, tn=_pick_tile(off_beta, 512), tm=tm,
                                 out_dtype=F32, transposed=True, name="mm_proj_dn")
        ba, ba_s = _matmul(h, hs, w_in_t, l, n_cols=LANES, tn=LANES, tm=tm, out_dtype=F32,
                           b_col0=off_beta, transposed=True, name="mm_proj_ba")
        proj3, proj3_s = _matmul(h, hs, w_in_t, l, n_cols=n3, tn=tn3, tm=tm, out_dtype=F32,
                                 b_col0=off_beta, shift=shift, transposed=True, name="mm_proj_da")

        o_dn, s_p = _delta_branch(proj1, ba, zero_buf, dn_conv[l], zero_state, dn_a_log[l], dn_dt_bias[l],
                                  dn_out_norm[l], batch=B, seq=L, valid=L, n_heads=dn_h)
        buf_s = jnp.concatenate([jnp.zeros((DB, CONV_PAD - (CONV_WIDTH - 1), 3 * dn_w), F32),
                                 state_conv[l]], axis=1)
        o_dn_sq, s_s = _delta_branch(proj1_s, ba_s, buf_s, dn_conv[l], state_delta[l],
                                     dn_a_log[l], dn_dt_bias[l], dn_out_norm[l],
                                     batch=DB, seq=seq_s, valid=1, n_heads=dn_h, single=True)
        o_dn_s = _pad_tail(o_dn_sq.reshape(DB, seq_s, dn_w)[:, 0])
        sp.append(s_p)
        ssm.append(s_s)
        cp.append(proj1.reshape(B, L, off_beta)[:, L - (CONV_WIDTH - 1):, :3 * dn_w])
        csm.append(jnp.concatenate([state_conv[l][:, 1:], proj1_s[:DB, None, :3 * dn_w]], axis=1))

        o_da, *kv_rows = _diff_attn_prompt(proj3, cos_p, sin_p, da_q_norm[l], da_k_norm[l],
                                           da_sub_norm[l], da_lambda[l], kv_rows, layer=l, depth=depth,
                                           batch=B, seq=L, n_heads=da_h, lam_init=lam_init, tq=tq, tk=tk)
        tail3 = proj3_s[:DB]
        to_sh = lambda t: t.reshape(DB, da_h, 2, DA_HEAD_DIM).transpose(0, 2, 1, 3).reshape(
            DB, 2 * da_h, DA_HEAD_DIM)
        v_new = tail3[:, 2 * da_w:3 * da_w].reshape(DB, da_h, 2 * DA_HEAD_DIM)
        o_da_sq, k_new = _diff_attn_sample(page_table, to_sh(tail3[:, :da_w]), to_sh(tail3[:, da_w:2 * da_w]),
                                           v_new, cos_s, sin_s, da_q_norm[l], da_k_norm[l], da_sub_norm[l],
                                           da_lambda[l], cache_k, cache_v, layer=l, lam_init=lam_init)
        o_da_s = _pad_tail(o_da_sq.reshape(DB, da_w))
        ksm.append(k_new.reshape(DB, 2, da_h, DA_HEAD_DIM).transpose(0, 2, 1, 3).reshape(
            DB, 1, da_h, 2 * DA_HEAD_DIM))
        vsm.append(v_new.reshape(DB, 1, da_h, 2 * DA_HEAD_DIM))

        tnd = _pick_tile(D, 512)
        merged, merged_s = _matmul_merge(o_dn, o_da, o_dn_s, o_da_s, w_branch_a, w_branch_b, l,
                                         proj3, proj3_s, gate_col0=3 * da_w, tn=tnd, tm=tm)
        x, xs = _matmul(merged, merged_s, w_out, l, n_cols=D, tn=tnd, tm=tm, out_dtype=F32,
                        resid=(x, xs), name="mm_out")
        hn, hns = _rmsnorm(x, xs, ffn_norm[l])
        act, act_s = _matmul_swiglu(hn, hns, w_gate_up, l, d_ff=d_ff, tn=_pick_tile(d_ff, 256), tm=tm)
        x, xs = _matmul(act, act_s, w_down_bf, l, n_cols=D, tn=tnd,
                        tm=_pick_tile(m, 512, unit=TAIL), out_dtype=F32, resid=(x, xs), name="mm_down")

    y_prompt = x.reshape(B, L, D)
    y_sample = xs[:DB].reshape(DB, 1, D)
    k_prompt, v_prompt = (t.reshape(depth, B, L, da_h, 2 * DA_HEAD_DIM) for t in kv_rows)
    return (y_prompt, y_sample, k_prompt, v_prompt, jnp.stack(ksm), jnp.stack(vsm),
            jnp.stack(sp), jnp.stack(ssm), jnp.stack(cp), jnp.stack(csm))
```

```python
import functools
import math

import jax
import jax.numpy as jnp
from jax import lax
from jax.experimental import pallas as pl
from jax.experimental.pallas import tpu as pltpu

F32 = jnp.float32
BF16 = jnp.bfloat16

RMS_EPS = 1e-6
ROPE_THETA = 10000.0
CONV_WIDTH = 4
DN_HEAD_DIM = 128
DN_CHUNK = 64
DA_HEAD_DIM = 128
LANES = 128
SUBLANES = 8
TAIL = 16
CONV_PAD = 8
VMEM_LIMIT = 56 * 1024 * 1024
DN_HEADS_PER_STEP = 2
PAGES_PER_STEP = 8


def _pick_tile(n, cap, unit=LANES):
    best = None
    t = unit
    while t <= min(n, cap):
        if n % t == 0:
            best = t
        t += unit
    assert best is not None, (n, cap)
    return best


def _nt_dot(a, b):
    return lax.dot_general(a, b, (((1,), (1,)), ((), ())), preferred_element_type=F32)


def _dot(a, b):
    return jnp.dot(a, b, preferred_element_type=F32)


def _silu(x):
    return x * jax.nn.sigmoid(x)


def _split3(x):
    hi = x.astype(BF16)
    r = x - hi.astype(F32)
    mid = r.astype(BF16)
    return hi, mid, (r - mid.astype(F32)).astype(BF16)


def _params(sem):
    return pltpu.CompilerParams(dimension_semantics=sem, vmem_limit_bytes=VMEM_LIMIT)


def _rmsnorm_kernel(x_ref, xs_ref, g_ref, o_ref, os_ref):
    def norm(x):
        ms = jnp.mean(x * x, axis=-1, keepdims=True)
        return (x * lax.rsqrt(ms + RMS_EPS) * g_ref[...]).astype(BF16)

    o_ref[...] = norm(x_ref[...])

    @pl.when(pl.program_id(0) == 0)
    def _():
        os_ref[...] = norm(xs_ref[...])


def _rmsnorm(x, xs, gain, tr=256):
    m, d = x.shape
    return pl.pallas_call(
        _rmsnorm_kernel,
        grid=(m // tr,),
        in_specs=[pl.BlockSpec((tr, d), lambda i: (i, 0)),
                  pl.BlockSpec((TAIL, d), lambda i: (0, 0)),
                  pl.BlockSpec((1, d), lambda i: (0, 0))],
        out_specs=[pl.BlockSpec((tr, d), lambda i: (i, 0)),
                   pl.BlockSpec((TAIL, d), lambda i: (0, 0))],
        out_shape=[jax.ShapeDtypeStruct((m, d), BF16), jax.ShapeDtypeStruct((TAIL, d), BF16)],
        compiler_params=_params(("arbitrary",)),
        name="rmsnorm",
    )(x, xs, gain.reshape(1, d))


CAST_ROWS = 256
CAST_COLS = 512


def _cast_weight(b_ref, bscr, b_next=None, shift=0):
    k, tn = bscr.shape
    step = CAST_ROWS if k % CAST_ROWS == 0 else k

    def body(c, carry):
        r0 = pl.multiple_of(c * step, step)
        w = b_ref[pl.ds(r0, step), :]
        if b_next is not None:
            w = jnp.concatenate([w, b_next[pl.ds(r0, step), :]], axis=1)[:, shift:shift + tn]
        bscr[pl.ds(r0, step), :] = w.astype(BF16)
        return carry

    lax.fori_loop(0, k // step, body, 0)


def _cast_weight_t(bt_ref, bscr, bt_next=None, shift=0):
    tn, k = bscr.shape
    step = CAST_COLS if k % CAST_COLS == 0 else k
    for c0 in range(0, k, step):
        cols = slice(c0, c0 + step)
        bscr[0:tn - shift, cols] = bt_ref[shift:tn, cols].astype(BF16)
        if shift:
            bscr[tn - shift:tn, cols] = bt_next[:, cols].astype(BF16)


def _mm_plain_kernel(*refs, shift, cast, resid, transposed):
    refs = list(refs)
    a_ref, as_ref, b_ref = refs[:3]
    del refs[:3]
    bn_ref = refs.pop(0) if shift else None
    r_ref, rs_ref = (refs.pop(0), refs.pop(0)) if resid else (None, None)
    o_ref, os_ref = refs[:2]
    w = refs[2] if cast else b_ref
    first = pl.program_id(1) == 0

    if cast:
        @pl.when(first)
        def _():
            (_cast_weight_t if transposed else _cast_weight)(b_ref, w, bn_ref, shift)

    def out(a, r):
        y = _nt_dot(a[...], w[...]) if transposed else _dot(a[...], w[...])
        return y if r is None else r[...] + y

    o_ref[...] = out(a_ref, r_ref).astype(o_ref.dtype)

    @pl.when(first)
    def _():
        os_ref[...] = out(as_ref, rs_ref).astype(os_ref.dtype)


def _matmul(a, a_s, b, layer, *, n_cols, tn, tm, out_dtype, b_col0=0, shift=0, resid=None, transposed=False,
            name="mm"):
    m, k = a.shape
    assert m % tm == 0 and b_col0 % tn == 0 and n_cols % tn == 0 and 0 <= shift < LANES
    joff = b_col0 // tn
    cast = b.dtype != BF16
    assert cast or not shift
    in_specs = [pl.BlockSpec((tm, k), lambda j, i: (i, 0)),
                pl.BlockSpec((TAIL, k), lambda j, i: (0, 0))]
    args = [a, a_s, b]
    if transposed:
        assert cast and shift % SUBLANES == 0 and (not shift or (tn % shift == 0 and b_col0 % shift == 0))
        in_specs.append(pl.BlockSpec((None, tn, k), lambda j, i: (layer, j + joff, 0)))
        if shift:
            per = tn // shift
            in_specs.append(pl.BlockSpec((None, shift, k), lambda j, i: (layer, (j + joff + 1) * per, 0)))
            args.append(b)
    else:
        in_specs.append(pl.BlockSpec((None, k, tn), lambda j, i: (layer, 0, j + joff)))
        if shift:
            per = tn // LANES
            in_specs.append(pl.BlockSpec((None, k, LANES), lambda j, i: (layer, 0, (j + joff + 1) * per)))
            args.append(b)
    if resid is not None:
        in_specs += [pl.BlockSpec((tm, tn), lambda j, i: (i, j)),
                     pl.BlockSpec((TAIL, tn), lambda j, i: (0, j))]
        args += list(resid)
    return pl.pallas_call(
        functools.partial(_mm_plain_kernel, shift=shift, cast=cast, resid=resid is not None,
                          transposed=transposed),
        grid=(n_cols // tn, m // tm),
        in_specs=in_specs,
        out_specs=[pl.BlockSpec((tm, tn), lambda j, i: (i, j)),
                   pl.BlockSpec((TAIL, tn), lambda j, i: (0, j))],
        out_shape=[jax.ShapeDtypeStruct((m, n_cols), out_dtype),
                   jax.ShapeDtypeStruct((TAIL, n_cols), out_dtype)],
        scratch_shapes=[pltpu.VMEM((tn, k) if transposed else (k, tn), BF16)] if cast else [],
        compiler_params=_params(("arbitrary", "arbitrary")),
        name=name,
    )(*args)


def _mm_swiglu_kernel(a_ref, as_ref, bg_ref, bu_ref, o_ref, os_ref, sg, su):
    first = pl.program_id(1) == 0

    @pl.when(first)
    def _():
        _cast_weight(bg_ref, sg)
        _cast_weight(bu_ref, su)

    def out(a):
        return (_silu(_dot(a, sg[...])) * _dot(a, su[...])).astype(BF16)

    o_ref[...] = out(a_ref[...])

    @pl.when(first)
    def _():
        os_ref[...] = out(as_ref[...])


def _matmul_swiglu(a, a_s, w_gate_up, layer, *, d_ff, tn, tm):
    m, k = a.shape
    uoff = d_ff // tn
    return pl.pallas_call(
        _mm_swiglu_kernel,
        grid=(d_ff // tn, m // tm),
        in_specs=[pl.BlockSpec((tm, k), lambda j, i: (i, 0)),
                  pl.BlockSpec((TAIL, k), lambda j, i: (0, 0)),
                  pl.BlockSpec((None, k, tn), lambda j, i: (layer, 0, j)),
                  pl.BlockSpec((None, k, tn), lambda j, i: (layer, 0, j + uoff))],
        out_specs=[pl.BlockSpec((tm, tn), lambda j, i: (i, j)),
                   pl.BlockSpec((TAIL, tn), lambda j, i: (0, j))],
        out_shape=[jax.ShapeDtypeStruct((m, d_ff), BF16), jax.ShapeDtypeStruct((TAIL, d_ff), BF16)],
        scratch_shapes=[pltpu.VMEM((k, tn), BF16), pltpu.VMEM((k, tn), BF16)],
        compiler_params=_params(("arbitrary", "arbitrary")),
        name="mm_swiglu",
    )(a, a_s, w_gate_up, w_gate_up)


def _mm_merge_kernel(a1_ref, a2_ref, a1s_ref, a2s_ref, b1_ref, b2_ref, g1_ref, g2_ref, g1s_ref, g2s_ref,
                     o_ref, os_ref, s1, s2):
    first = pl.program_id(1) == 0

    @pl.when(first)
    def _():
        _cast_weight(b1_ref, s1)
        _cast_weight(b2_ref, s2)

    def out(a1, a2, g1, g2):
        return (jax.nn.sigmoid(g1[...]) * _dot(a1[...], s1[...])
                + jax.nn.sigmoid(g2[...]) * _dot(a2[...], s2[...])).astype(BF16)

    o_ref[...] = out(a1_ref, a2_ref, g1_ref, g2_ref)

    @pl.when(first)
    def _():
        os_ref[...] = out(a1s_ref, a2s_ref, g1s_ref, g2s_ref)


def _matmul_merge(a1, a2, a1s, a2s, b1, b2, layer, gates, gates_s, *, gate_col0, tn, tm):
    m, k = a1.shape
    n = b1.shape[2]
    assert gate_col0 % tn == 0
    g1off = gate_col0 // tn
    g2off = g1off + n // tn
    row = lambda blk: pl.BlockSpec((tm, blk), lambda j, i: (i, 0))
    tail = lambda blk: pl.BlockSpec((TAIL, blk), lambda j, i: (0, 0))
    wt = pl.BlockSpec((None, k, tn), lambda j, i: (layer, 0, j))
    return pl.pallas_call(
        _mm_merge_kernel,
        grid=(n // tn, m // tm),
        in_specs=[row(k), row(k), tail(k), tail(k), wt, wt,
                  pl.BlockSpec((tm, tn), lambda j, i: (i, j + g1off)),
                  pl.BlockSpec((tm, tn), lambda j, i: (i, j + g2off)),
                  pl.BlockSpec((TAIL, tn), lambda j, i: (0, j + g1off)),
                  pl.BlockSpec((TAIL, tn), lambda j, i: (0, j + g2off))],
        out_specs=[pl.BlockSpec((tm, tn), lambda j, i: (i, j)),
                   pl.BlockSpec((TAIL, tn), lambda j, i: (0, j))],
        out_shape=[jax.ShapeDtypeStruct((m, n), BF16), jax.ShapeDtypeStruct((TAIL, n), BF16)],
        scratch_shapes=[pltpu.VMEM((k, tn), BF16), pltpu.VMEM((k, tn), BF16)],
        compiler_params=_params(("arbitrary", "arbitrary")),
        name="mm_merge",
    )(a1, a2, a1s, a2s, b1, b2, gates, gates, gates_s, gates_s)


DN_SUPER = 4
DN_INTERLEAVE = 2


def _delta_kernel(hpar_ref,
                  qp_ref, kp_ref, vp_ref, z_ref, ba_ref,
                  cbq_ref, cbk_ref, cbv_ref, cwq_ref, cwk_ref, cwv_ref,
                  s0_ref, gn_ref,
                  o_ref, sn_ref,
                  xq_s, xk_s, xv_s, aq_s, n_s, oc_s, gt_s, mf_s, mb_s,
                  *, seq, valid, n_heads, single):
    C = DN_CHUNK
    D = DN_HEAD_DIM
    P = DN_HEADS_PER_STEP
    hp = pl.program_id(1)
    n_chunks = seq // C
    U = min(DN_SUPER, n_chunks)
    R = U * C
    AQ = D + C

    def rows(ref, r0, n):
        if not single:
            return ref[pl.ds(r0, n), :]
        assert n == seq
        first = lax.broadcasted_iota(jnp.int32, (seq, 1), 0) == 0
        return jnp.where(first, ref[pl.ds(pl.program_id(0), 1), :], 0.0)

    for xs, cb, xp in ((xq_s, cbq_ref, qp_ref), (xk_s, cbk_ref, kp_ref), (xv_s, cbv_ref, vp_ref)):
        xs[0:CONV_PAD, :] = cb[...]
        xs[CONV_PAD:, :] = rows(xp, 0, seq)

    row = lax.broadcasted_iota(jnp.int32, (R, R), 0)
    col = lax.broadcasted_iota(jnp.int32, (R, R), 1)
    same = (row // C) == (col // C)
    mf_s[0] = (same & (row >= col)).astype(F32)
    mf_s[1] = (same & (row > col)).astype(F32)
    mf_s[2] = (row == col).astype(F32)
    mb_s[0] = (same & (row >= col)).astype(BF16)
    mb_s[1] = (same & (row <= col)).astype(BF16)
    mb_s[2] = same.astype(BF16)
    ones_bf = jnp.ones((R, R), BF16)
    lane = lax.broadcasted_iota(jnp.int32, (R, LANES), 1)

    def conv_silu(xs, cw_ref, r0):
        xc = xs[pl.ds(r0, R + CONV_PAD), :]
        w = cw_ref[...]
        first = CONV_PAD - (CONV_WIDTH - 1)
        y = xc[first:first + R] * w[0:1]
        for t in range(1, CONV_WIDTH):
            y = y + xc[first + t:first + t + R] * w[t:t + 1]
        return _silu(y)

    def l2n(x):
        return x * lax.rsqrt(jnp.sum(x * x, axis=-1, keepdims=True) + RMS_EPS)

    def exact_dot(sel_bf, terms):
        return sum(_dot(sel_bf, t) for t in terms)

    def each(fn, *lists):
        return [fn(*args) for args in zip(*lists)]

    S = DN_INTERLEAVE if (n_chunks // U) % DN_INTERLEAVE == 0 else 1
    units = [(s, p) for s in range(S) for p in range(P)]

    def prep_stages(it):
        r0s = [pl.multiple_of((it * S + s) * R, R) for s in range(S)]
        qc = [conv_silu(xq_s, cwq_ref, r0) for r0 in r0s]
        kc = [conv_silu(xk_s, cwk_ref, r0) for r0 in r0s]
        vc = [conv_silu(xv_s, cwv_ref, r0) for r0 in r0s]
        ba = [rows(ba_ref, r0, R) for r0 in r0s]
        yield
        q = [l2n(qc[s][:, p * D:(p + 1) * D]) * (D ** -0.5) for s, p in units]
        k = [l2n(kc[s][:, p * D:(p + 1) * D]) for s, p in units]
        v = [vc[s][:, p * D:(p + 1) * D] for s, p in units]
        neg_a = -jnp.exp(hpar_ref[0:1, :])
        beta_all = [jax.nn.sigmoid(t) for t in ba]
        xs_all = [t + hpar_ref[1:2, :] for t in ba]
        g_all = [neg_a * (jnp.maximum(x, 0.0) + jnp.log1p(jnp.exp(-jnp.abs(x)))) for x in xs_all]
        beta, g = [], []
        for s, p in units:
            h = hp * P + p
            beta.append(jnp.sum(jnp.where(lane == h, beta_all[s], 0.0), axis=-1, keepdims=True))
            g.append(jnp.sum(jnp.where(lane == n_heads + h, g_all[s], 0.0), axis=-1, keepdims=True))
        if valid < seq:
            live = [(r0s[s] + lax.broadcasted_iota(jnp.int32, (R, 1), 0)) < valid for s, _ in units]
            dead = lambda t, keep: jnp.where(keep, t, 0.0)
            k, v, beta, g = each(dead, k, live), each(dead, v, live), each(dead, beta, live), each(dead, g, live)
        yield

        widen = lambda t: jnp.concatenate([t] * (R // D), axis=1) if R > D else t[:, :R]
        g3 = each(lambda t: _split3(jnp.broadcast_to(t, (R, D))), g)
        gsum = each(lambda t3: exact_dot(mb_s[0], t3), g3)
        g_last = each(lambda t3: exact_dot(mb_s[2], t3), g3)
        gj = each(lambda t3: exact_dot(ones_bf, [widen(t) * mb_s[1] for t in t3]), g3)
        yield
        decay = each(lambda a, b: jnp.exp(jnp.where(mf_s[0] > 0.5, widen(a) - b, -jnp.inf)), gsum, gj)
        kbf = each(lambda t: t.astype(BF16), k)
        m = each(lambda kb, d, b: _nt_dot(kb, kb) * d * b * mf_s[1], kbf, decay, beta)
        yield
        x_pow = each(lambda t: -t, m)
        t_inv = each(lambda t: mf_s[2] + t, x_pow)
        for _ in range(int(math.log2(C)) - 1):
            xb = each(lambda t: t.astype(BF16), x_pow)
            x_pow = each(lambda t: _dot(t, t), xb)
            t_inv = each(lambda t, xp: t + _dot(t.astype(BF16), xp.astype(BF16)), t_inv, x_pow)
            yield
        m_hi = each(lambda t: t.astype(BF16), m)
        m_lo = each(lambda t, hi: (t - hi.astype(F32)).astype(BF16), m, m_hi)
        t_hi = each(lambda t: t.astype(BF16), t_inv)
        t_lo = each(lambda t, hi: (t - hi.astype(F32)).astype(BF16), t_inv, t_hi)
        resid = each(lambda t, mh, ml, th, tl: mf_s[2] - t - (_dot(mh, th) + (_dot(mh, tl) + _dot(ml, th))),
                     t_inv, m_hi, m_lo, t_hi, t_lo)
        yield
        t_inv = each(lambda t, th, r: t + _dot(th, r.astype(BF16)), t_inv, t_hi, resid)

        yield
        e_g = each(jnp.exp, gsum)
        wu = each(lambda t, kk, vv, b, e: _dot(t.astype(BF16), jnp.concatenate(
            [kk * (b * e), vv * b], axis=1).astype(BF16)).astype(BF16), t_inv, k, v, beta, e_g)
        yield
        qk = each(lambda qq, kb, d: (_nt_dot(qq.astype(BF16), kb) * d).astype(BF16), q, kbf, decay)
        qwo = each(_dot, qk, wu)
        k_dec = each(lambda kk, gl, gs: (kk * jnp.exp(gl - gs)).astype(BF16), k, g_last, gsum)
        g_tot = each(jnp.exp, g_last)
        q_eff = each(lambda qq, e, t: (qq * e - t[:, :D]).astype(BF16), q, e_g, qwo)
        yield
        for i, (s, p) in enumerate(units):
            oc_s[p, pl.ds(r0s[s], R), :] = qwo[i][:, D:]
        for u in range(U):
            rws = slice(u * C, (u + 1) * C)
            an = each(lambda kd, w: lax.dot_general(kd[rws], w[rws], (((0,), (0,)), ((), ())),
                                                    preferred_element_type=F32), k_dec, wu)
            for i, (s, p) in enumerate(units):
                c = (it * S + s) * U + u
                base = pl.multiple_of(c * (P * AQ) + p * AQ, SUBLANES)
                aq_s[pl.ds(base, D), :] = an[i][:, :D].astype(BF16)
                aq_s[pl.ds(base + D, C), :] = q_eff[i][rws]
                n_s[p, pl.ds(pl.multiple_of(c * D, D), D), :] = an[i][:, D:]
                gt_s[p, pl.ds(pl.multiple_of(c * SUBLANES, SUBLANES), SUBLANES), :] = (
                    g_tot[i][u * C:u * C + SUBLANES])
            yield

    gain = gn_ref[...]

    def scan(c, s):
        r0 = pl.multiple_of(c * C, C)
        z = rows(z_ref, r0, C)
        r = _dot(aq_s[pl.ds(pl.multiple_of(c * (P * AQ), SUBLANES), P * AQ), :], s.astype(BF16))
        new_s = []
        outs = []
        for p in range(P):
            rp = r[p * AQ:(p + 1) * AQ, p * D:(p + 1) * D]
            g_tot = gt_s[p, pl.ds(pl.multiple_of(c * SUBLANES, SUBLANES), 1), :]
            new_s.append(s[:, p * D:(p + 1) * D] * g_tot - rp[:D]
                         + n_s[p, pl.ds(pl.multiple_of(c * D, D), D), :])
            o = rp[D:] + oc_s[p, pl.ds(r0, C), :]
            ms = jnp.mean(o * o, axis=-1, keepdims=True)
            y = o * lax.rsqrt(ms + RMS_EPS) * gain
            outs.append(y * _silu(z[:, p * D:(p + 1) * D]))
        o_ref[pl.ds(r0, C), :] = jnp.concatenate(outs, axis=1).astype(o_ref.dtype)
        return jnp.concatenate(new_s, axis=1)

    group = U * S
    n_groups = n_chunks // group

    def prep_with_scan(it, s):
        c0 = (it - 1) * group
        done = 0
        for _ in prep_stages(it):
            if done < group:
                s = scan(c0 + done, s)
                done += 1
        for c in range(done, group):
            s = scan(c0 + c, s)
        return s

    for _ in prep_stages(jnp.int32(0)):
        pass
    state = jnp.concatenate([s0_ref[p] for p in range(P)], axis=1)
    state = lax.fori_loop(1, n_groups, prep_with_scan, state)
    final = lax.fori_loop((n_groups - 1) * group, n_chunks, scan, state)
    for p in range(P):
        sn_ref[p] = final[:, p * D:(p + 1) * D]


def _delta_branch(proj1, ba, conv_buf8, conv_w, s0, a_log, dt_bias, out_gain, *, batch, seq, valid, n_heads,
                  single=False):
    D = DN_HEAD_DIM
    C = DN_CHUNK
    H = n_heads
    P = DN_HEADS_PER_STEP
    G = H // P
    PD = P * D
    assert H % P == 0
    n_chunks = seq // C
    assert n_chunks % min(DN_SUPER, n_chunks) == 0
    rr = min(DN_SUPER, n_chunks) * C
    if single:
        assert valid == 1 and seq == C
        blk = lambda off: pl.BlockSpec((TAIL, PD), lambda b, h: (0, off + h))
        ba_spec = pl.BlockSpec((TAIL, LANES), lambda b, h: (0, 0))
    else:
        blk = lambda off: pl.BlockSpec((seq, PD), lambda b, h: (b, off + h))
        ba_spec = pl.BlockSpec((seq, LANES), lambda b, h: (b, 0))
    cbs = lambda off: pl.BlockSpec((None, CONV_PAD, PD), lambda b, h: (b, 0, off + h))
    cws = lambda off: pl.BlockSpec((CONV_WIDTH, PD), lambda b, h: (0, off + h))
    hpar = jnp.zeros((SUBLANES, LANES), F32).at[0, H:2 * H].set(a_log).at[1, H:2 * H].set(dt_bias)
    o, s_new = pl.pallas_call(
        functools.partial(_delta_kernel, seq=seq, valid=valid, n_heads=H, single=single),
        grid=(batch, G),
        in_specs=[pl.BlockSpec((SUBLANES, LANES), lambda b, h: (0, 0)),
                  blk(0), blk(G), blk(2 * G), blk(3 * G), ba_spec,
                  cbs(0), cbs(G), cbs(2 * G), cws(0), cws(G), cws(2 * G),
                  pl.BlockSpec((None, P, D, D), lambda b, h: (b, h, 0, 0)),
                  pl.BlockSpec((1, D), lambda b, h: (0, 0))],
        out_specs=[pl.BlockSpec((seq, PD), lambda b, h: (b, h)),
                   pl.BlockSpec((None, P, D, D), lambda b, h: (b, h, 0, 0))],
        out_shape=[jax.ShapeDtypeStruct((batch * seq, H * D), BF16),
                   jax.ShapeDtypeStruct((batch, H, D, D), F32)],
        scratch_shapes=[pltpu.VMEM((seq + CONV_PAD, PD), F32)] * 3 + [
            pltpu.VMEM((n_chunks * P * (D + C), D), BF16),
            pltpu.VMEM((P, n_chunks * D, D), F32),
            pltpu.VMEM((P, seq, D), F32),
            pltpu.VMEM((P, n_chunks * SUBLANES, LANES), F32),
            pltpu.VMEM((3, rr, rr), F32), pltpu.VMEM((3, rr, rr), BF16)],
        compiler_params=_params(("arbitrary", "arbitrary")),
        name="delta",
    )(hpar, proj1, proj1, proj1, proj1, ba,
      conv_buf8, conv_buf8, conv_buf8, conv_w, conv_w, conv_w, s0, out_gain.reshape(1, D))
    return o, s_new


def _norm_rope(x, gain, cos, sin):
    ms = jnp.mean(x * x, axis=-1, keepdims=True)
    y = x * lax.rsqrt(ms + RMS_EPS) * gain
    return y * cos + pltpu.roll(y, DA_HEAD_DIM // 2, 1) * sin


def _lambda(lp, lam_init):
    return (jnp.exp(jnp.sum(lp[0:1] * lp[1:2], axis=1, keepdims=True))
            - jnp.exp(jnp.sum(lp[2:3] * lp[3:4], axis=1, keepdims=True)) + lam_init)


def _da_kernel(q_ref, k_ref, v_ref, cos_ref, sin_ref, qg_ref, kg_ref, sg_ref, lp_ref, *refs,
               seq, tq, tk, lam_init):
    o_ref, krow_ref, vrow_ref, kbf_s, vbf_s = refs[-5:]
    dh = DA_HEAD_DIM
    qi = pl.program_id(2)
    per = tq // tk
    scale = dh ** -0.5

    @pl.when(qi == 0)
    def _():
        kg = kg_ref[...]

        def body(c, carry):
            r0 = pl.multiple_of(c * tk, tk)
            kb = k_ref[pl.ds(r0, tk), :]
            cos = cos_ref[pl.ds(r0, tk), :]
            sin = sin_ref[pl.ds(r0, tk), :]
            kr = jnp.concatenate([_norm_rope(kb[:, :dh], kg, cos, sin),
                                  _norm_rope(kb[:, dh:], kg, cos, sin)], axis=1)
            krow_ref[pl.ds(r0, tk), :] = kr
            kbf_s[pl.ds(r0, tk), :] = kr.astype(BF16)
            vb = v_ref[pl.ds(r0, tk), :]
            vrow_ref[pl.ds(r0, tk), :] = vb
            vbf_s[pl.ds(r0, tk), :] = vb.astype(BF16)
            return carry

        lax.fori_loop(0, seq // tk, body, 0)

    q0 = pl.multiple_of(qi * tq, tq)
    qb = q_ref[...]
    cos = cos_ref[pl.ds(q0, tq), :]
    sin = sin_ref[pl.ds(q0, tq), :]
    qg = qg_ref[...]
    q1 = _norm_rope(qb[:, :dh], qg, cos, sin).astype(BF16)
    q2 = _norm_rope(qb[:, dh:], qg, cos, sin).astype(BF16)
    lam = _lambda(lp_ref[...], lam_init)
    q_pos = q0 + lax.broadcasted_iota(jnp.int32, (tq, tk), 0)
    k_off = lax.broadcasted_iota(jnp.int32, (tq, tk), 1)

    def scores(kt, masked):
        kb = kbf_s[pl.ds(pl.multiple_of(kt * tk, tk), tk), :]
        s1 = _nt_dot(q1, kb[:, :dh]) * scale
        s2 = _nt_dot(q2, kb[:, dh:]) * scale
        if masked:
            visible = kt * tk + k_off <= q_pos
            s1 = jnp.where(visible, s1, -jnp.inf)
            s2 = jnp.where(visible, s2, -jnp.inf)
        return s1, s2

    def fold(x, op):
        y = x[:, :LANES]
        for c in range(1, tk // LANES):
            y = op(y, x[:, c * LANES:(c + 1) * LANES])
        return y

    def over_tiles(step, carry):
        def group(j, c):
            for u in range(per):
                c = step(j * per + u, c, False)
            return c

        carry = lax.fori_loop(0, qi, group, carry)
        for u in range(per):
            carry = step(qi * per + u, carry, True)
        return carry

    def max_step(kt, carry, masked):
        s1, s2 = scores(kt, masked)
        return jnp.maximum(carry[0], fold(s1, jnp.maximum)), jnp.maximum(carry[1], fold(s2, jnp.maximum))

    neg = jnp.full((tq, LANES), -jnp.inf, F32)
    mx = over_tiles(max_step, (neg, neg))
    m1 = jnp.max(mx[0], axis=-1, keepdims=True)
    m2 = jnp.max(mx[1], axis=-1, keepdims=True)

    def sum_step(kt, carry, masked):
        l1, l2, a1, a2 = carry
        s1, s2 = scores(kt, masked)
        e1 = jnp.exp(s1 - m1)
        e2 = jnp.exp(s2 - m2)
        vb = vbf_s[pl.ds(pl.multiple_of(kt * tk, tk), tk), :]
        return (l1 + fold(e1, jnp.add), l2 + fold(e2, jnp.add),
                a1 + _dot(e1.astype(BF16), vb), a2 + _dot(e2.astype(BF16), vb))

    zl = jnp.zeros((tq, LANES), F32)
    za = jnp.zeros((tq, 2 * dh), F32)
    l1, l2, a1, a2 = over_tiles(sum_step, (zl, zl, za, za))
    inv1 = 1.0 / jnp.sum(l1, axis=-1, keepdims=True)
    inv2 = 1.0 / jnp.sum(l2, axis=-1, keepdims=True)
    o = a1 * inv1 - lam * (a2 * inv2)
    ms = jnp.mean(o * o, axis=-1, keepdims=True)
    o_ref[...] = (o * lax.rsqrt(ms + RMS_EPS) * sg_ref[...] * (1.0 - lam_init)).astype(o_ref.dtype)


def _diff_attn_prompt(proj3, cos, sin, q_gain, k_gain, sub_gain, lam_p, kv_rows, *, layer, depth, batch, seq,
                      n_heads, lam_init, tq, tk):
    dh = DA_HEAD_DIM
    H = n_heads
    nq = seq // tq
    full = lambda shape: pl.BlockSpec(shape, lambda b, h, qi: (0, 0))
    n_in = 9
    carried = [] if kv_rows is None else list(kv_rows)
    rows_spec = pl.BlockSpec((None, seq, 2 * dh), lambda b, h, qi: (layer, b, h))
    rows_shape = jax.ShapeDtypeStruct((depth, batch * seq, H * 2 * dh), F32)
    return pl.pallas_call(
        functools.partial(_da_kernel, seq=seq, tq=tq, tk=tk, lam_init=lam_init),
        grid=(batch, H, nq),
        in_specs=[pl.BlockSpec((tq, 2 * dh), lambda b, h, qi: (b * nq + qi, h)),
                  pl.BlockSpec((seq, 2 * dh), lambda b, h, qi: (b, H + h)),
                  pl.BlockSpec((seq, 2 * dh), lambda b, h, qi: (b, 2 * H + h)),
                  full((seq, dh)), full((seq, dh)), full((1, dh)), full((1, dh)),
                  full((1, 2 * dh)), full((4, dh))] + [pl.BlockSpec(memory_space=pl.ANY)] * len(carried),
        out_specs=[pl.BlockSpec((tq, 2 * dh), lambda b, h, qi: (b * nq + qi, h)), rows_spec, rows_spec],
        out_shape=[jax.ShapeDtypeStruct((batch * seq, H * 2 * dh), BF16), rows_shape, rows_shape],
        input_output_aliases={n_in + i: 1 + i for i in range(len(carried))},
        scratch_shapes=[pltpu.VMEM((seq, 2 * dh), BF16), pltpu.VMEM((seq, 2 * dh), BF16)],
        compiler_params=_params(("arbitrary", "arbitrary", "arbitrary")),
        name="diff_attn_prompt",
    )(proj3, proj3, proj3, cos, sin, q_gain.reshape(1, dh), k_gain.reshape(1, dh),
      sub_gain.reshape(1, 2 * dh), lam_p, *carried)


def _decode_kernel(pt_ref, q_ref, kn_ref, vn_ref, cos_ref, sin_ref, qg_ref, kg_ref, sg_ref, lp_ref,
                   *refs, n_steps, pages_per_step, n_heads, lam_init):
    G = pages_per_step
    kc_refs, vc_refs = refs[:G], refs[G:2 * G]
    o_ref, kout_ref, qmat_s, sc_s, a_s, acc_s, new_s = refs[2 * G:]
    dh = DA_HEAD_DIM
    H = n_heads
    R = 2 * H
    page = kc_refs[0].shape[0]
    PH = page * H
    p = pl.program_id(1)
    scale = dh ** -0.5
    own_head = (lax.broadcasted_iota(jnp.int32, (R, PH), 1) % H
                == lax.broadcasted_iota(jnp.int32, (R, PH), 0) % H)

    @pl.when(p == 0)
    def _():
        cos = cos_ref[...]
        sin = sin_ref[...]
        qn = _norm_rope(q_ref[...], qg_ref[...], cos, sin)
        kn = _norm_rope(kn_ref[...], kg_ref[...], cos, sin)
        kout_ref[...] = kn
        sub = lax.broadcasted_iota(jnp.int32, (R, 2 * dh), 0) // H
        half = lax.broadcasted_iota(jnp.int32, (R, 2 * dh), 1) // dh
        qmat_s[...] = jnp.where(sub == half, jnp.concatenate([qn, qn], axis=1), 0.0).astype(BF16)
        s_new = jnp.sum(qn.astype(BF16).astype(F32) * kn.astype(BF16).astype(F32),
                        axis=-1, keepdims=True) * scale
        new_s[...] = jnp.broadcast_to(s_new, (R, LANES))
        acc_s[...] = jnp.zeros_like(acc_s)

    @pl.when(p < n_steps)
    def _():
        for g in range(G):
            kp = kc_refs[g][...].reshape(PH, 2 * dh).astype(BF16)
            s = _nt_dot(qmat_s[...], kp) * scale
            sc_s[p * G + g] = jnp.where(own_head, s, -jnp.inf)

    @pl.when(p == n_steps - 1)
    def _():
        n_pages = n_steps * G
        s_new = new_s[...]
        unroll = 8 if n_pages % 8 == 0 else 1

        def pages(fn, init):
            def group(j, acc):
                for u in range(unroll):
                    acc = fn(j * unroll + u, acc)
                return acc
            return lax.fori_loop(0, n_pages // unroll, group, init)

        m_el = pages(lambda i, acc: jnp.maximum(acc, sc_s[i]), jnp.full((R, PH), -jnp.inf, F32))
        m = jnp.maximum(jnp.max(m_el, axis=-1, keepdims=True), s_new[:, 0:1])
        l_el = pages(lambda i, acc: acc + jnp.exp(sc_s[i] - m), jnp.zeros((R, PH), F32))
        e_new = jnp.exp(s_new - m)
        l = jnp.sum(l_el, axis=-1, keepdims=True) + e_new[:, 0:1]
        inv = 1.0 / l
        lam = _lambda(lp_ref[...], lam_init)

        def weights(i, carry):
            pr = jnp.exp(sc_s[i] - m) * inv
            a = pr[:H] - lam * pr[H:]
            a_s[i] = jnp.concatenate([a, jnp.zeros_like(a)], axis=0).astype(BF16)
            return carry

        pages(weights, 0)
        pn = e_new * inv
        new_s[...] = jnp.concatenate([pn[:H] - lam * pn[H:], jnp.zeros((R - H, LANES), F32)], axis=0)

    @pl.when(p >= n_steps)
    def _():
        acc = acc_s[...]
        for g in range(G):
            vp = vc_refs[g][...].reshape(PH, 2 * dh).astype(BF16)
            acc = acc + _dot(a_s[(p - n_steps) * G + g], vp)
        acc_s[...] = acc

    @pl.when(p == 2 * n_steps - 1)
    def _():
        a_new = new_s[...][:H, 0:1].astype(BF16).astype(F32)
        o = acc_s[...][:H] + a_new * vn_ref[...].astype(BF16).astype(F32)
        ms = jnp.mean(o * o, axis=-1, keepdims=True)
        o_ref[...] = (o * lax.rsqrt(ms + RMS_EPS) * sg_ref[...] * (1.0 - lam_init)).astype(o_ref.dtype)


def _diff_attn_sample(page_table, q_sh, k_sh, v_new, cos, sin, q_gain, k_gain, sub_gain, lam_p,
                      cache_k, cache_v, *, layer, lam_init):
    dh = DA_HEAD_DIM
    B, n_pages = page_table.shape
    G = max(g for g in range(1, PAGES_PER_STEP + 1) if n_pages % g == 0)
    H = v_new.shape[1]
    R = 2 * H
    page = cache_k.shape[2]
    n_steps = n_pages // G
    per_b = lambda shape: pl.BlockSpec((None,) + shape, lambda b, p, pt: (b, 0, 0))
    full = lambda shape: pl.BlockSpec(shape, lambda b, p, pt: (0, 0))

    def k_page(g):
        return pl.BlockSpec((None, None, page, H, 2 * dh),
                            lambda b, p, pt: (layer, pt[b, jnp.minimum(p, n_steps - 1) * G + g], 0, 0, 0))

    def v_page(g):
        return pl.BlockSpec((None, None, page, H, 2 * dh),
                            lambda b, p, pt: (layer, pt[b, jnp.maximum(p - n_steps, 0) * G + g], 0, 0, 0))

    grid_spec = pltpu.PrefetchScalarGridSpec(
        num_scalar_prefetch=1,
        grid=(B, 2 * n_steps),
        in_specs=[per_b((R, dh)), per_b((R, dh)), per_b((H, 2 * dh)),
                  full((1, dh)), full((1, dh)), full((1, dh)), full((1, dh)),
                  full((1, 2 * dh)), full((4, dh))]
                 + [k_page(g) for g in range(G)] + [v_page(g) for g in range(G)],
        out_specs=[per_b((H, 2 * dh)), per_b((R, dh))],
        scratch_shapes=[pltpu.VMEM((R, 2 * dh), BF16),
                        pltpu.VMEM((n_pages, R, page * H), F32),
                        pltpu.VMEM((n_pages, R, page * H), BF16),
                        pltpu.VMEM((R, 2 * dh), F32),
                        pltpu.VMEM((R, LANES), F32)])
    o, k_out = pl.pallas_call(
        functools.partial(_decode_kernel, n_steps=n_steps, pages_per_step=G, n_heads=H, lam_init=lam_init),
        grid_spec=grid_spec,
        out_shape=[jax.ShapeDtypeStruct((B, H, 2 * dh), BF16),
                   jax.ShapeDtypeStruct((B, R, dh), F32)],
        compiler_params=_params(("arbitrary", "arbitrary")),
        name="diff_attn_sample",
    )(page_table, q_sh, k_sh, v_new, cos, sin, q_gain.reshape(1, dh), k_gain.reshape(1, dh),
      sub_gain.reshape(1, 2 * dh), lam_p, *([cache_k] * G), *([cache_v] * G))
    return o, k_out


def _rope_tables(pos):
    half = DA_HEAD_DIM // 2
    inv_freq = ROPE_THETA ** (-jnp.arange(half, dtype=F32) / half)
    ang = pos.astype(F32)[:, None] * inv_freq[None, :]
    cos = jnp.cos(ang)
    sin = jnp.sin(ang)
    return jnp.concatenate([cos, cos], axis=1), jnp.concatenate([-sin, sin], axis=1)


def _pad_tail(t):
    return jnp.concatenate([t, jnp.zeros((TAIL - t.shape[0],) + t.shape[1:], t.dtype)], axis=0)


def kernel(x_prompt, x_sample, cache_k, cache_v, state_delta, state_conv, page_table, attn_norm, w_in, dn_conv, dn_a_log, dn_dt_bias, dn_out_norm, da_q_norm, da_k_norm, da_lambda, da_sub_norm, w_branch_a, w_branch_b, w_out, ffn_norm, w_gate_up, w_down):
    B, L, D = x_prompt.shape
    DB = x_sample.shape[0]
    depth = w_in.shape[0]
    assert x_sample.shape[1] == 1 and DB <= TAIL and L % DN_CHUNK == 0
    n_pages = page_table.shape[1]
    page = cache_k.shape[2]
    past_len = n_pages * page
    dn_w = dn_conv.shape[2] // 3
    dn_h = dn_w // DN_HEAD_DIM
    da_h = cache_k.shape[3]
    da_w = da_h * 2 * DA_HEAD_DIM
    d_ff = w_down.shape[1]
    off_beta = 4 * dn_w
    shift = 2 * dn_h
    n3 = 3 * da_w + 2 * D
    assert shift < LANES and w_in.shape[2] == off_beta + shift + n3

    m = B * L
    tm = _pick_tile(m, 1024, unit=TAIL)
    x = x_prompt.reshape(m, D)
    xs = _pad_tail(x_sample.reshape(DB, D))
    cos_p, sin_p = _rope_tables(jnp.arange(L))
    cos_s, sin_s = _rope_tables(past_len + jnp.arange(1))
    tq = _pick_tile(L, 512)
    tk = _pick_tile(tq, 256)
    zero_buf = jnp.zeros((B, CONV_PAD, 3 * dn_w), F32)
    zero_state = jnp.zeros((B, dn_h, DN_HEAD_DIM, DN_HEAD_DIM), F32)
    seq_s = DN_CHUNK
    tn3 = _pick_tile(math.gcd(n3, off_beta), 512)
    w_down_bf = w_down.astype(BF16)
    w_in_t = jnp.swapaxes(w_in, 1, 2)

    ksm, vsm, sp, ssm, cp, csm = [], [], [], [], [], []
    kv_rows = None
    for l in range(depth):
        lam_init = 0.8 - 0.6 * math.exp(-0.3 * l)
        h, hs = _rmsnorm(x, xs, attn_norm[l])
        proj1, proj1_s = _matmul(h, hs, w_in_t, l, n_cols=off_beta, tn=_pick_tile(off_beta, 512), tm=tm,
                                 out_dtype=F32, transposed=True, name="mm_proj_dn")
        ba, ba_s = _matmul(h, hs, w_in_t, l, n_cols=LANES, tn=LANES, tm=tm, out_dtype=F32,
                           b_col0=off_beta, transposed=True, name="mm_proj_ba")
        proj3, proj3_s = _matmul(h, hs, w_in_t, l, n_cols=n3, tn=tn3, tm=tm, out_dtype=F32,
                                 b_col0=off_beta, shift=shift, transposed=True, name="mm_proj_da")

        o_dn, s_p = _delta_branch(proj1, ba, zero_buf, dn_conv[l], zero_state, dn_a_log[l], dn_dt_bias[l],
                                  dn_out_norm[l], batch=B, seq=L, valid=L, n_heads=dn_h)
        buf_s = jnp.concatenate([jnp.zeros((DB, CONV_PAD - (CONV_WIDTH - 1), 3 * dn_w), F32),
                                 state_conv[l]], axis=1)
        o_dn_sq, s_s = _delta_branch(proj1_s, ba_s, buf_s, dn_conv[l], state_delta[l],
                                     dn_a_log[l], dn_dt_bias[l], dn_out_norm[l],
                                     batch=DB, seq=seq_s, valid=1, n_heads=dn_h, single=True)
        o_dn_s = _pad_tail(o_dn_sq.reshape(DB, seq_s, dn_w)[:, 0])
        sp.append(s_p)
        ssm.append(s_s)
        cp.append(proj1.reshape(B, L, off_beta)[:, L - (CONV_WIDTH - 1):, :3 * dn_w])
        csm.append(jnp.concatenate([state_conv[l][:, 1:], proj1_s[:DB, None, :3 * dn_w]], axis=1))

        o_da, *kv_rows = _diff_attn_prompt(proj3, cos_p, sin_p, da_q_norm[l], da_k_norm[l],
                                           da_sub_norm[l], da_lambda[l], kv_rows, layer=l, depth=depth,
                                           batch=B, seq=L, n_heads=da_h, lam_init=lam_init, tq=tq, tk=tk)
        tail3 = proj3_s[:DB]
        to_sh = lambda t: t.reshape(DB, da_h, 2, DA_HEAD_DIM).transpose(0, 2, 1, 3).reshape(
            DB, 2 * da_h, DA_HEAD_DIM)
        v_new = tail3[:, 2 * da_w:3 * da_w].reshape(DB, da_h, 2 * DA_HEAD_DIM)
        o_da_sq, k_new = _diff_attn_sample(page_table, to_sh(tail3[:, :da_w]), to_sh(tail3[:, da_w:2 * da_w]),
                                           v_new, cos_s, sin_s, da_q_norm[l], da_k_norm[l], da_sub_norm[l],
                                           da_lambda[l], cache_k, cache_v, layer=l, lam_init=lam_init)
        o_da_s = _pad_tail(o_da_sq.reshape(DB, da_w))
        ksm.append(k_new.reshape(DB, 2, da_h, DA_HEAD_DIM).transpose(0, 2, 1, 3).reshape(
            DB, 1, da_h, 2 * DA_HEAD_DIM))
        vsm.append(v_new.reshape(DB, 1, da_h, 2 * DA_HEAD_DIM))

        tnd = _pick_tile(D, 512)
        merged, merged_s = _matmul_merge(o_dn, o_da, o_dn_s, o_da_s, w_branch_a, w_branch_b, l,
                                         proj3, proj3_s, gate_col0=3 * da_w, tn=tnd, tm=tm)
        x, xs = _matmul(merged, merged_s, w_out, l, n_cols=D, tn=tnd, tm=tm, out_dtype=F32,
                        resid=(x, xs), name="mm_out")
        hn, hns = _rmsnorm(x, xs, ffn_norm[l])
        act, act_s = _matmul_swiglu(hn, hns, w_gate_up, l, d_ff=d_ff, tn=_pick_tile(d_ff, 256), tm=tm)
        x, xs = _matmul(act, act_s, w_down_bf, l, n_cols=D, tn=tnd,
                        tm=_pick_tile(m, 512, unit=TAIL), out_dtype=F32, resid=(x, xs), name="mm_down")

    y_prompt = x.reshape(B, L, D)
    y_sample = xs[:DB].reshape(DB, 1, D)
    k_prompt, v_prompt = (t.reshape(depth, B, L, da_h, 2 * DA_HEAD_DIM) for t in kv_rows)
    return (y_prompt, y_sample, k_prompt, v_prompt, jnp.stack(ksm), jnp.stack(vsm),
            jnp.stack(sp), jnp.stack(ssm), jnp.stack(cp), jnp.stack(csm))
```

```python
import functools
import math

import jax
import jax.numpy as jnp
from jax import lax
from jax.experimental import pallas as pl
from jax.experimental.pallas import tpu as pltpu

F32 = jnp.float32
BF16 = jnp.bfloat16

RMS_EPS = 1e-6
ROPE_THETA = 10000.0
CONV_WIDTH = 4
DN_HEAD_DIM = 128
DN_CHUNK = 64
DA_HEAD_DIM = 128
LANES = 128
SUBLANES = 8
TAIL = 16
CONV_PAD = 8
VMEM_LIMIT = 56 * 1024 * 1024
DN_HEADS_PER_STEP = 2
DN_HEADS_PER_STEP_SINGLE = 8
PAGES_PER_STEP = 8


def _pick_tile(n, cap, unit=LANES):
    best = None
    t = unit
    while t <= min(n, cap):
        if n % t == 0:
            best = t
        t += unit
    assert best is not None, (n, cap)
    return best


def _nt_dot(a, b):
    return lax.dot_general(a, b, (((1,), (1,)), ((), ())), preferred_element_type=F32)


def _dot(a, b):
    return jnp.dot(a, b, preferred_element_type=F32)


def _silu(x):
    return x * jax.nn.sigmoid(x)


def _split3(x):
    hi = x.astype(BF16)
    r = x - hi.astype(F32)
    mid = r.astype(BF16)
    return hi, mid, (r - mid.astype(F32)).astype(BF16)


def _params(sem):
    return pltpu.CompilerParams(dimension_semantics=sem, vmem_limit_bytes=VMEM_LIMIT)


def _rmsnorm_kernel(x_ref, xs_ref, g_ref, o_ref, os_ref):
    def norm(x):
        ms = jnp.mean(x * x, axis=-1, keepdims=True)
        return (x * lax.rsqrt(ms + RMS_EPS) * g_ref[...]).astype(BF16)

    o_ref[...] = norm(x_ref[...])

    @pl.when(pl.program_id(0) == 0)
    def _():
        os_ref[...] = norm(xs_ref[...])


def _rmsnorm(x, xs, gain, tr=256):
    m, d = x.shape
    return pl.pallas_call(
        _rmsnorm_kernel,
        grid=(m // tr,),
        in_specs=[pl.BlockSpec((tr, d), lambda i: (i, 0)),
                  pl.BlockSpec((TAIL, d), lambda i: (0, 0)),
                  pl.BlockSpec((1, d), lambda i: (0, 0))],
        out_specs=[pl.BlockSpec((tr, d), lambda i: (i, 0)),
                   pl.BlockSpec((TAIL, d), lambda i: (0, 0))],
        out_shape=[jax.ShapeDtypeStruct((m, d), BF16), jax.ShapeDtypeStruct((TAIL, d), BF16)],
        compiler_params=_params(("arbitrary",)),
        name="rmsnorm",
    )(x, xs, gain.reshape(1, d))


CAST_ROWS = 256
CAST_COLS = 512


def _cast_weight(b_ref, bscr, b_next=None, shift=0):
    k, tn = bscr.shape
    step = CAST_ROWS if k % CAST_ROWS == 0 else k

    def body(c, carry):
        r0 = pl.multiple_of(c * step, step)
        w = b_ref[pl.ds(r0, step), :]
        if b_next is not None:
            w = jnp.concatenate([w, b_next[pl.ds(r0, step), :]], axis=1)[:, shift:shift + tn]
        bscr[pl.ds(r0, step), :] = w.astype(BF16)
        return carry

    lax.fori_loop(0, k // step, body, 0)


def _cast_weight_t(bt_ref, bscr, bt_next=None, shift=0):
    tn, k = bscr.shape
    step = CAST_COLS if k % CAST_COLS == 0 else k
    for c0 in range(0, k, step):
        cols = slice(c0, c0 + step)
        bscr[0:tn - shift, cols] = bt_ref[shift:tn, cols].astype(BF16)
        if shift:
            bscr[tn - shift:tn, cols] = bt_next[:, cols].astype(BF16)


def _mm_plain_kernel(*refs, shift, cast, resid, transposed):
    refs = list(refs)
    a_ref, as_ref, b_ref = refs[:3]
    del refs[:3]
    bn_ref = refs.pop(0) if shift else None
    r_ref, rs_ref = (refs.pop(0), refs.pop(0)) if resid else (None, None)
    o_ref, os_ref = refs[:2]
    w = refs[2] if cast else b_ref
    first = pl.program_id(1) == 0

    if cast:
        @pl.when(first)
        def _():
            (_cast_weight_t if transposed else _cast_weight)(b_ref, w, bn_ref, shift)

    def out(a, r):
        y = _nt_dot(a[...], w[...]) if transposed else _dot(a[...], w[...])
        return y if r is None else r[...] + y

    o_ref[...] = out(a_ref, r_ref).astype(o_ref.dtype)

    @pl.when(first)
    def _():
        os_ref[...] = out(as_ref, rs_ref).astype(os_ref.dtype)


def _matmul(a, a_s, b, layer, *, n_cols, tn, tm, out_dtype, b_col0=0, shift=0, resid=None, transposed=False,
            name="mm"):
    m, k = a.shape
    assert m % tm == 0 and b_col0 % tn == 0 and n_cols % tn == 0 and 0 <= shift < LANES
    joff = b_col0 // tn
    cast = b.dtype != BF16
    assert cast or not shift
    in_specs = [pl.BlockSpec((tm, k), lambda j, i: (i, 0)),
                pl.BlockSpec((TAIL, k), lambda j, i: (0, 0))]
    args = [a, a_s, b]
    if transposed:
        assert cast and shift % SUBLANES == 0 and (not shift or (tn % shift == 0 and b_col0 % shift == 0))
        in_specs.append(pl.BlockSpec((None, tn, k), lambda j, i: (layer, j + joff, 0)))
        if shift:
            per = tn // shift
            in_specs.append(pl.BlockSpec((None, shift, k), lambda j, i: (layer, (j + joff + 1) * per, 0)))
            args.append(b)
    else:
        in_specs.append(pl.BlockSpec((None, k, tn), lambda j, i: (layer, 0, j + joff)))
        if shift:
            per = tn // LANES
            in_specs.append(pl.BlockSpec((None, k, LANES), lambda j, i: (layer, 0, (j + joff + 1) * per)))
            args.append(b)
    if resid is not None:
        in_specs += [pl.BlockSpec((tm, tn), lambda j, i: (i, j)),
                     pl.BlockSpec((TAIL, tn), lambda j, i: (0, j))]
        args += list(resid)
    return pl.pallas_call(
        functools.partial(_mm_plain_kernel, shift=shift, cast=cast, resid=resid is not None,
                          transposed=transposed),
        grid=(n_cols // tn, m // tm),
        in_specs=in_specs,
        out_specs=[pl.BlockSpec((tm, tn), lambda j, i: (i, j)),
                   pl.BlockSpec((TAIL, tn), lambda j, i: (0, j))],
        out_shape=[jax.ShapeDtypeStruct((m, n_cols), out_dtype),
                   jax.ShapeDtypeStruct((TAIL, n_cols), out_dtype)],
        scratch_shapes=[pltpu.VMEM((tn, k) if transposed else (k, tn), BF16)] if cast else [],
        compiler_params=_params(("arbitrary", "arbitrary")),
        name=name,
    )(*args)


def _mm_swiglu_kernel(a_ref, as_ref, bg_ref, bu_ref, o_ref, os_ref, sg, su):
    first = pl.program_id(1) == 0

    @pl.when(first)
    def _():
        _cast_weight(bg_ref, sg)
        _cast_weight(bu_ref, su)

    def out(a):
        return (_silu(_dot(a, sg[...])) * _dot(a, su[...])).astype(BF16)

    o_ref[...] = out(a_ref[...])

    @pl.when(first)
    def _():
        os_ref[...] = out(as_ref[...])


def _matmul_swiglu(a, a_s, w_gate_up, layer, *, d_ff, tn, tm):
    m, k = a.shape
    uoff = d_ff // tn
    return pl.pallas_call(
        _mm_swiglu_kernel,
        grid=(d_ff // tn, m // tm),
        in_specs=[pl.BlockSpec((tm, k), lambda j, i: (i, 0)),
                  pl.BlockSpec((TAIL, k), lambda j, i: (0, 0)),
                  pl.BlockSpec((None, k, tn), lambda j, i: (layer, 0, j)),
                  pl.BlockSpec((None, k, tn), lambda j, i: (layer, 0, j + uoff))],
        out_specs=[pl.BlockSpec((tm, tn), lambda j, i: (i, j)),
                   pl.BlockSpec((TAIL, tn), lambda j, i: (0, j))],
        out_shape=[jax.ShapeDtypeStruct((m, d_ff), BF16), jax.ShapeDtypeStruct((TAIL, d_ff), BF16)],
        scratch_shapes=[pltpu.VMEM((k, tn), BF16), pltpu.VMEM((k, tn), BF16)],
        compiler_params=_params(("arbitrary", "arbitrary")),
        name="mm_swiglu",
    )(a, a_s, w_gate_up, w_gate_up)


def _mm_merge_kernel(a1_ref, a2_ref, a1s_ref, a2s_ref, b1_ref, b2_ref, g1_ref, g2_ref, g1s_ref, g2s_ref,
                     o_ref, os_ref, s1, s2):
    first = pl.program_id(1) == 0

    @pl.when(first)
    def _():
        _cast_weight(b1_ref, s1)
        _cast_weight(b2_ref, s2)

    def out(a1, a2, g1, g2):
        return (jax.nn.sigmoid(g1[...]) * _dot(a1[...], s1[...])
                + jax.nn.sigmoid(g2[...]) * _dot(a2[...], s2[...])).astype(BF16)

    o_ref[...] = out(a1_ref, a2_ref, g1_ref, g2_ref)

    @pl.when(first)
    def _():
        os_ref[...] = out(a1s_ref, a2s_ref, g1s_ref, g2s_ref)


def _matmul_merge(a1, a2, a1s, a2s, b1, b2, layer, gates, gates_s, *, gate_col0, tn, tm):
    m, k = a1.shape
    n = b1.shape[2]
    assert gate_col0 % tn == 0
    g1off = gate_col0 // tn
    g2off = g1off + n // tn
    row = lambda blk: pl.BlockSpec((tm, blk), lambda j, i: (i, 0))
    tail = lambda blk: pl.BlockSpec((TAIL, blk), lambda j, i: (0, 0))
    wt = pl.BlockSpec((None, k, tn), lambda j, i: (layer, 0, j))
    return pl.pallas_call(
        _mm_merge_kernel,
        grid=(n // tn, m // tm),
        in_specs=[row(k), row(k), tail(k), tail(k), wt, wt,
                  pl.BlockSpec((tm, tn), lambda j, i: (i, j + g1off)),
                  pl.BlockSpec((tm, tn), lambda j, i: (i, j + g2off)),
                  pl.BlockSpec((TAIL, tn), lambda j, i: (0, j + g1off)),
                  pl.BlockSpec((TAIL, tn), lambda j, i: (0, j + g2off))],
        out_specs=[pl.BlockSpec((tm, tn), lambda j, i: (i, j)),
                   pl.BlockSpec((TAIL, tn), lambda j, i: (0, j))],
        out_shape=[jax.ShapeDtypeStruct((m, n), BF16), jax.ShapeDtypeStruct((TAIL, n), BF16)],
        scratch_shapes=[pltpu.VMEM((k, tn), BF16), pltpu.VMEM((k, tn), BF16)],
        compiler_params=_params(("arbitrary", "arbitrary")),
        name="mm_merge",
    )(a1, a2, a1s, a2s, b1, b2, gates, gates, gates_s, gates_s)


DN_SUPER = 4
DN_INTERLEAVE = 2


def _delta_kernel(hpar_ref,
                  qp_ref, kp_ref, vp_ref, z_ref, ba_ref,
                  cbq_ref, cbk_ref, cbv_ref, cwq_ref, cwk_ref, cwv_ref,
                  s0_ref, gn_ref,
                  o_ref, sn_ref,
                  xq_s, xk_s, xv_s, aq_s, n_s, oc_s, gt_s, mf_s, mb_s,
                  *, seq, valid, n_heads, heads_per_step, single):
    C = DN_CHUNK
    D = DN_HEAD_DIM
    P = heads_per_step
    hp = pl.program_id(1)
    n_chunks = seq // C
    U = min(DN_SUPER, n_chunks)
    R = U * C
    AQ = D + C

    def rows(ref, r0, n):
        if not single:
            return ref[pl.ds(r0, n), :]
        assert n == seq
        first = lax.broadcasted_iota(jnp.int32, (seq, 1), 0) == 0
        return jnp.where(first, ref[pl.ds(pl.program_id(0), 1), :], 0.0)

    for xs, cb, xp in ((xq_s, cbq_ref, qp_ref), (xk_s, cbk_ref, kp_ref), (xv_s, cbv_ref, vp_ref)):
        xs[0:CONV_PAD, :] = cb[...]
        xs[CONV_PAD:, :] = rows(xp, 0, seq)

    row = lax.broadcasted_iota(jnp.int32, (R, R), 0)
    col = lax.broadcasted_iota(jnp.int32, (R, R), 1)
    same = (row // C) == (col // C)
    mf_s[0] = (same & (row >= col)).astype(F32)
    mf_s[1] = (same & (row > col)).astype(F32)
    mf_s[2] = (row == col).astype(F32)
    mb_s[0] = (same & (row >= col)).astype(BF16)
    mb_s[1] = (same & (row <= col)).astype(BF16)
    mb_s[2] = same.astype(BF16)
    ones_bf = jnp.ones((R, R), BF16)
    lane = lax.broadcasted_iota(jnp.int32, (R, LANES), 1)

    def conv_silu(xs, cw_ref, r0):
        xc = xs[pl.ds(r0, R + CONV_PAD), :]
        w = cw_ref[...]
        first = CONV_PAD - (CONV_WIDTH - 1)
        y = xc[first:first + R] * w[0:1]
        for t in range(1, CONV_WIDTH):
            y = y + xc[first + t:first + t + R] * w[t:t + 1]
        return _silu(y)

    def l2n(x):
        return x * lax.rsqrt(jnp.sum(x * x, axis=-1, keepdims=True) + RMS_EPS)

    def exact_dot(sel_bf, terms):
        return sum(_dot(sel_bf, t) for t in terms)

    def each(fn, *lists):
        return [fn(*args) for args in zip(*lists)]

    S = DN_INTERLEAVE if (n_chunks // U) % DN_INTERLEAVE == 0 else 1
    units = [(s, p) for s in range(S) for p in range(P)]

    def prep_stages(it):
        r0s = [pl.multiple_of((it * S + s) * R, R) for s in range(S)]
        qc = [conv_silu(xq_s, cwq_ref, r0) for r0 in r0s]
        kc = [conv_silu(xk_s, cwk_ref, r0) for r0 in r0s]
        vc = [conv_silu(xv_s, cwv_ref, r0) for r0 in r0s]
        ba = [rows(ba_ref, r0, R) for r0 in r0s]
        yield
        q = [l2n(qc[s][:, p * D:(p + 1) * D]) * (D ** -0.5) for s, p in units]
        k = [l2n(kc[s][:, p * D:(p + 1) * D]) for s, p in units]
        v = [vc[s][:, p * D:(p + 1) * D] for s, p in units]
        neg_a = -jnp.exp(hpar_ref[0:1, :])
        beta_all = [jax.nn.sigmoid(t) for t in ba]
        xs_all = [t + hpar_ref[1:2, :] for t in ba]
        g_all = [neg_a * (jnp.maximum(x, 0.0) + jnp.log1p(jnp.exp(-jnp.abs(x)))) for x in xs_all]
        beta, g = [], []
        for s, p in units:
            h = hp * P + p
            beta.append(jnp.sum(jnp.where(lane == h, beta_all[s], 0.0), axis=-1, keepdims=True))
            g.append(jnp.sum(jnp.where(lane == n_heads + h, g_all[s], 0.0), axis=-1, keepdims=True))
        if valid < seq:
            live = [(r0s[s] + lax.broadcasted_iota(jnp.int32, (R, 1), 0)) < valid for s, _ in units]
            dead = lambda t, keep: jnp.where(keep, t, 0.0)
            k, v, beta, g = each(dead, k, live), each(dead, v, live), each(dead, beta, live), each(dead, g, live)
        yield

        widen = lambda t: jnp.concatenate([t] * (R // D), axis=1) if R > D else t[:, :R]
        g3 = each(lambda t: _split3(jnp.broadcast_to(t, (R, D))), g)
        gsum = each(lambda t3: exact_dot(mb_s[0], t3), g3)
        g_last = each(lambda t3: exact_dot(mb_s[2], t3), g3)
        gj = each(lambda t3: exact_dot(ones_bf, [widen(t) * mb_s[1] for t in t3]), g3)
        yield
        decay = each(lambda a, b: jnp.exp(jnp.where(mf_s[0] > 0.5, widen(a) - b, -jnp.inf)), gsum, gj)
        kbf = each(lambda t: t.astype(BF16), k)
        m = each(lambda kb, d, b: _nt_dot(kb, kb) * d * b * mf_s[1], kbf, decay, beta)
        yield
        x_pow = each(lambda t: -t, m)
        t_inv = each(lambda t: mf_s[2] + t, x_pow)
        for _ in range(int(math.log2(C)) - 1):
            xb = each(lambda t: t.astype(BF16), x_pow)
            x_pow = each(lambda t: _dot(t, t), xb)
            t_inv = each(lambda t, xp: t + _dot(t.astype(BF16), xp.astype(BF16)), t_inv, x_pow)
            yield
        m_hi = each(lambda t: t.astype(BF16), m)
        m_lo = each(lambda t, hi: (t - hi.astype(F32)).astype(BF16), m, m_hi)
        t_hi = each(lambda t: t.astype(BF16), t_inv)
        t_lo = each(lambda t, hi: (t - hi.astype(F32)).astype(BF16), t_inv, t_hi)
        resid = each(lambda t, mh, ml, th, tl: mf_s[2] - t - (_dot(mh, th) + (_dot(mh, tl) + _dot(ml, th))),
                     t_inv, m_hi, m_lo, t_hi, t_lo)
        yield
        t_inv = each(lambda t, th, r: t + _dot(th, r.astype(BF16)), t_inv, t_hi, resid)

        yield
        e_g = each(jnp.exp, gsum)
        wu = each(lambda t, kk, vv, b, e: _dot(t.astype(BF16), jnp.concatenate(
            [kk * (b * e), vv * b], axis=1).astype(BF16)).astype(BF16), t_inv, k, v, beta, e_g)
        yield
        qk = each(lambda qq, kb, d: (_nt_dot(qq.astype(BF16), kb) * d).astype(BF16), q, kbf, decay)
        qwo = each(_dot, qk, wu)
        k_dec = each(lambda kk, gl, gs: (kk * jnp.exp(gl - gs)).astype(BF16), k, g_last, gsum)
        g_tot = each(jnp.exp, g_last)
        q_eff = each(lambda qq, e, t: (qq * e - t[:, :D]).astype(BF16), q, e_g, qwo)
        yield
        for i, (s, p) in enumerate(units):
            oc_s[p, pl.ds(r0s[s], R), :] = qwo[i][:, D:]
        for u in range(U):
            rws = slice(u * C, (u + 1) * C)
            an = each(lambda kd, w: lax.dot_general(kd[rws], w[rws], (((0,), (0,)), ((), ())),
                                                    preferred_element_type=F32), k_dec, wu)
            for i, (s, p) in enumerate(units):
                c = (it * S + s) * U + u
                base = pl.multiple_of(c * (P * AQ) + p * AQ, SUBLANES)
                aq_s[pl.ds(base, D), :] = an[i][:, :D].astype(BF16)
                aq_s[pl.ds(base + D, C), :] = q_eff[i][rws]
                n_s[p, pl.ds(pl.multiple_of(c * D, D), D), :] = an[i][:, D:]
                gt_s[p, pl.ds(pl.multiple_of(c * SUBLANES, SUBLANES), SUBLANES), :] = (
                    g_tot[i][u * C:u * C + SUBLANES])
            yield

    gain = gn_ref[...]

    def scan(c, s):
        r0 = pl.multiple_of(c * C, C)
        z = rows(z_ref, r0, C)
        r = _dot(aq_s[pl.ds(pl.multiple_of(c * (P * AQ), SUBLANES), P * AQ), :], s.astype(BF16))
        new_s = []
        outs = []
        for p in range(P):
            rp = r[p * AQ:(p + 1) * AQ, p * D:(p + 1) * D]
            g_tot = gt_s[p, pl.ds(pl.multiple_of(c * SUBLANES, SUBLANES), 1), :]
            new_s.append(s[:, p * D:(p + 1) * D] * g_tot - rp[:D]
                         + n_s[p, pl.ds(pl.multiple_of(c * D, D), D), :])
            o = rp[D:] + oc_s[p, pl.ds(r0, C), :]
            ms = jnp.mean(o * o, axis=-1, keepdims=True)
            y = o * lax.rsqrt(ms + RMS_EPS) * gain
            outs.append(y * _silu(z[:, p * D:(p + 1) * D]))
        o_ref[pl.ds(r0, C), :] = jnp.concatenate(outs, axis=1).astype(o_ref.dtype)
        return jnp.concatenate(new_s, axis=1)

    group = U * S
    n_groups = n_chunks // group

    def prep_with_scan(it, s):
        c0 = (it - 1) * group
        done = 0
        for _ in prep_stages(it):
            if done < group:
                s = scan(c0 + done, s)
                done += 1
        for c in range(done, group):
            s = scan(c0 + c, s)
        return s

    for _ in prep_stages(jnp.int32(0)):
        pass
    state = jnp.concatenate([s0_ref[p] for p in range(P)], axis=1)
    state = lax.fori_loop(1, n_groups, prep_with_scan, state)
    final = lax.fori_loop((n_groups - 1) * group, n_chunks, scan, state)
    for p in range(P):
        sn_ref[p] = final[:, p * D:(p + 1) * D]


def _delta_branch(proj1, ba, conv_buf8, conv_w, s0, a_log, dt_bias, out_gain, *, batch, seq, valid, n_heads,
                  single=False):
    D = DN_HEAD_DIM
    C = DN_CHUNK
    H = n_heads
    P = min(DN_HEADS_PER_STEP_SINGLE, H) if single else DN_HEADS_PER_STEP
    G = H // P
    PD = P * D
    assert H % P == 0
    n_chunks = seq // C
    assert n_chunks % min(DN_SUPER, n_chunks) == 0
    rr = min(DN_SUPER, n_chunks) * C
    if single:
        assert valid == 1 and seq == C
        blk = lambda off: pl.BlockSpec((TAIL, PD), lambda b, h: (0, off + h))
        ba_spec = pl.BlockSpec((TAIL, LANES), lambda b, h: (0, 0))
    else:
        blk = lambda off: pl.BlockSpec((seq, PD), lambda b, h: (b, off + h))
        ba_spec = pl.BlockSpec((seq, LANES), lambda b, h: (b, 0))
    cbs = lambda off: pl.BlockSpec((None, CONV_PAD, PD), lambda b, h: (b, 0, off + h))
    cws = lambda off: pl.BlockSpec((CONV_WIDTH, PD), lambda b, h: (0, off + h))
    hpar = jnp.zeros((SUBLANES, LANES), F32).at[0, H:2 * H].set(a_log).at[1, H:2 * H].set(dt_bias)
    o, s_new = pl.pallas_call(
        functools.partial(_delta_kernel, seq=seq, valid=valid, n_heads=H, heads_per_step=P, single=single),
        grid=(batch, G),
        in_specs=[pl.BlockSpec((SUBLANES, LANES), lambda b, h: (0, 0)),
                  blk(0), blk(G), blk(2 * G), blk(3 * G), ba_spec,
                  cbs(0), cbs(G), cbs(2 * G), cws(0), cws(G), cws(2 * G),
                  pl.BlockSpec((None, P, D, D), lambda b, h: (b, h, 0, 0)),
                  pl.BlockSpec((1, D), lambda b, h: (0, 0))],
        out_specs=[pl.BlockSpec((seq, PD), lambda b, h: (b, h)),
                   pl.BlockSpec((None, P, D, D), lambda b, h: (b, h, 0, 0))],
        out_shape=[jax.ShapeDtypeStruct((batch * seq, H * D), BF16),
                   jax.ShapeDtypeStruct((batch, H, D, D), F32)],
        scratch_shapes=[pltpu.VMEM((seq + CONV_PAD, PD), F32)] * 3 + [
            pltpu.VMEM((n_chunks * P * (D + C), D), BF16),
            pltpu.VMEM((P, n_chunks * D, D), F32),
            pltpu.VMEM((P, seq, D), F32),
            pltpu.VMEM((P, n_chunks * SUBLANES, LANES), F32),
            pltpu.VMEM((3, rr, rr), F32), pltpu.VMEM((3, rr, rr), BF16)],
        compiler_params=_params(("arbitrary", "arbitrary")),
        name="delta",
    )(hpar, proj1, proj1, proj1, proj1, ba,
      conv_buf8, conv_buf8, conv_buf8, conv_w, conv_w, conv_w, s0, out_gain.reshape(1, D))
    return o, s_new


def _norm_rope(x, gain, cos, sin):
    ms = jnp.mean(x * x, axis=-1, keepdims=True)
    y = x * lax.rsqrt(ms + RMS_EPS) * gain
    return y * cos + pltpu.roll(y, DA_HEAD_DIM // 2, 1) * sin


def _lambda(lp, lam_init):
    return (jnp.exp(jnp.sum(lp[0:1] * lp[1:2], axis=1, keepdims=True))
            - jnp.exp(jnp.sum(lp[2:3] * lp[3:4], axis=1, keepdims=True)) + lam_init)


def _da_kernel(q_ref, k_ref, v_ref, cos_ref, sin_ref, qg_ref, kg_ref, sg_ref, lp_ref, *refs,
               seq, tq, tk, lam_init):
    o_ref, krow_ref, vrow_ref, kbf_s, vbf_s = refs[-5:]
    dh = DA_HEAD_DIM
    qi = pl.program_id(2)
    per = tq // tk
    scale = dh ** -0.5

    @pl.when(qi == 0)
    def _():
        kg = kg_ref[...]

        def body(c, carry):
            r0 = pl.multiple_of(c * tk, tk)
            kb = k_ref[pl.ds(r0, tk), :]
            cos = cos_ref[pl.ds(r0, tk), :]
            sin = sin_ref[pl.ds(r0, tk), :]
            kr = jnp.concatenate([_norm_rope(kb[:, :dh], kg, cos, sin),
                                  _norm_rope(kb[:, dh:], kg, cos, sin)], axis=1)
            krow_ref[pl.ds(r0, tk), :] = kr
            kbf_s[pl.ds(r0, tk), :] = kr.astype(BF16)
            vb = v_ref[pl.ds(r0, tk), :]
            vrow_ref[pl.ds(r0, tk), :] = vb
            vbf_s[pl.ds(r0, tk), :] = vb.astype(BF16)
            return carry

        lax.fori_loop(0, seq // tk, body, 0)

    q0 = pl.multiple_of(qi * tq, tq)
    qb = q_ref[...]
    cos = cos_ref[pl.ds(q0, tq), :]
    sin = sin_ref[pl.ds(q0, tq), :]
    qg = qg_ref[...]
    q1 = _norm_rope(qb[:, :dh], qg, cos, sin).astype(BF16)
    q2 = _norm_rope(qb[:, dh:], qg, cos, sin).astype(BF16)
    lam = _lambda(lp_ref[...], lam_init)
    q_pos = q0 + lax.broadcasted_iota(jnp.int32, (tq, tk), 0)
    k_off = lax.broadcasted_iota(jnp.int32, (tq, tk), 1)

    def scores(kt, masked):
        kb = kbf_s[pl.ds(pl.multiple_of(kt * tk, tk), tk), :]
        s1 = _nt_dot(q1, kb[:, :dh]) * scale
        s2 = _nt_dot(q2, kb[:, dh:]) * scale
        if masked:
            visible = kt * tk + k_off <= q_pos
            s1 = jnp.where(visible, s1, -jnp.inf)
            s2 = jnp.where(visible, s2, -jnp.inf)
        return s1, s2

    def fold(x, op):
        y = x[:, :LANES]
        for c in range(1, tk // LANES):
            y = op(y, x[:, c * LANES:(c + 1) * LANES])
        return y

    def over_tiles(step, carry):
        def group(j, c):
            for u in range(per):
                c = step(j * per + u, c, False)
            return c

        carry = lax.fori_loop(0, qi, group, carry)
        for u in range(per):
            carry = step(qi * per + u, carry, True)
        return carry

    def max_step(kt, carry, masked):
        s1, s2 = scores(kt, masked)
        return jnp.maximum(carry[0], fold(s1, jnp.maximum)), jnp.maximum(carry[1], fold(s2, jnp.maximum))

    neg = jnp.full((tq, LANES), -jnp.inf, F32)
    mx = over_tiles(max_step, (neg, neg))
    m1 = jnp.max(mx[0], axis=-1, keepdims=True)
    m2 = jnp.max(mx[1], axis=-1, keepdims=True)

    def sum_step(kt, carry, masked):
        l1, l2, a1, a2 = carry
        s1, s2 = scores(kt, masked)
        e1 = jnp.exp(s1 - m1)
        e2 = jnp.exp(s2 - m2)
        vb = vbf_s[pl.ds(pl.multiple_of(kt * tk, tk), tk), :]
        return (l1 + fold(e1, jnp.add), l2 + fold(e2, jnp.add),
                a1 + _dot(e1.astype(BF16), vb), a2 + _dot(e2.astype(BF16), vb))

    zl = jnp.zeros((tq, LANES), F32)
    za = jnp.zeros((tq, 2 * dh), F32)
    l1, l2, a1, a2 = over_tiles(sum_step, (zl, zl, za, za))
    inv1 = 1.0 / jnp.sum(l1, axis=-1, keepdims=True)
    inv2 = 1.0 / jnp.sum(l2, axis=-1, keepdims=True)
    o = a1 * inv1 - lam * (a2 * inv2)
    ms = jnp.mean(o * o, axis=-1, keepdims=True)
    o_ref[...] = (o * lax.rsqrt(ms + RMS_EPS) * sg_ref[...] * (1.0 - lam_init)).astype(o_ref.dtype)


def _diff_attn_prompt(proj3, cos, sin, q_gain, k_gain, sub_gain, lam_p, kv_rows, *, layer, depth, batch, seq,
                      n_heads, lam_init, tq, tk):
    dh = DA_HEAD_DIM
    H = n_heads
    nq = seq // tq
    full = lambda shape: pl.BlockSpec(shape, lambda b, h, qi: (0, 0))
    n_in = 9
    carried = [] if kv_rows is None else list(kv_rows)
    rows_spec = pl.BlockSpec((None, seq, 2 * dh), lambda b, h, qi: (layer, b, h))
    rows_shape = jax.ShapeDtypeStruct((depth, batch * seq, H * 2 * dh), F32)
    return pl.pallas_call(
        functools.partial(_da_kernel, seq=seq, tq=tq, tk=tk, lam_init=lam_init),
        grid=(batch, H, nq),
        in_specs=[pl.BlockSpec((tq, 2 * dh), lambda b, h, qi: (b * nq + qi, h)),
                  pl.BlockSpec((seq, 2 * dh), lambda b, h, qi: (b, H + h)),
                  pl.BlockSpec((seq, 2 * dh), lambda b, h, qi: (b, 2 * H + h)),
                  full((seq, dh)), full((seq, dh)), full((1, dh)), full((1, dh)),
                  full((1, 2 * dh)), full((4, dh))] + [pl.BlockSpec(memory_space=pl.ANY)] * len(carried),
        out_specs=[pl.BlockSpec((tq, 2 * dh), lambda b, h, qi: (b * nq + qi, h)), rows_spec, rows_spec],
        out_shape=[jax.ShapeDtypeStruct((batch * seq, H * 2 * dh), BF16), rows_shape, rows_shape],
        input_output_aliases={n_in + i: 1 + i for i in range(len(carried))},
        scratch_shapes=[pltpu.VMEM((seq, 2 * dh), BF16), pltpu.VMEM((seq, 2 * dh), BF16)],
        compiler_params=_params(("arbitrary", "arbitrary", "arbitrary")),
        name="diff_attn_prompt",
    )(proj3, proj3, proj3, cos, sin, q_gain.reshape(1, dh), k_gain.reshape(1, dh),
      sub_gain.reshape(1, 2 * dh), lam_p, *carried)


def _decode_kernel(pt_ref, q_ref, kn_ref, vn_ref, cos_ref, sin_ref, qg_ref, kg_ref, sg_ref, lp_ref,
                   *refs, n_steps, pages_per_step, n_heads, lam_init):
    G = pages_per_step
    kc_refs, vc_refs = refs[:G], refs[G:2 * G]
    o_ref, kout_ref, qmat_s, sc_s, a_s, acc_s, new_s = refs[2 * G:]
    dh = DA_HEAD_DIM
    H = n_heads
    R = 2 * H
    page = kc_refs[0].shape[0]
    PH = page * H
    p = pl.program_id(1)
    scale = dh ** -0.5
    own_head = (lax.broadcasted_iota(jnp.int32, (R, PH), 1) % H
                == lax.broadcasted_iota(jnp.int32, (R, PH), 0) % H)

    @pl.when(p == 0)
    def _():
        cos = cos_ref[...]
        sin = sin_ref[...]
        qn = _norm_rope(q_ref[...], qg_ref[...], cos, sin)
        kn = _norm_rope(kn_ref[...], kg_ref[...], cos, sin)
        kout_ref[...] = kn
        sub = lax.broadcasted_iota(jnp.int32, (R, 2 * dh), 0) // H
        half = lax.broadcasted_iota(jnp.int32, (R, 2 * dh), 1) // dh
        qmat_s[...] = jnp.where(sub == half, jnp.concatenate([qn, qn], axis=1), 0.0).astype(BF16)
        s_new = jnp.sum(qn.astype(BF16).astype(F32) * kn.astype(BF16).astype(F32),
                        axis=-1, keepdims=True) * scale
        new_s[...] = jnp.broadcast_to(s_new, (R, LANES))
        acc_s[...] = jnp.zeros_like(acc_s)

    @pl.when(p < n_steps)
    def _():
        for g in range(G):
            kp = kc_refs[g][...].reshape(PH, 2 * dh).astype(BF16)
            s = _nt_dot(qmat_s[...], kp) * scale
            sc_s[p * G + g] = jnp.where(own_head, s, -jnp.inf)

    @pl.when(p == n_steps - 1)
    def _():
        n_pages = n_steps * G
        s_new = new_s[...]
        unroll = 8 if n_pages % 8 == 0 else 1

        def pages(fn, init):
            def group(j, acc):
                for u in range(unroll):
                    acc = fn(j * unroll + u, acc)
                return acc
            return lax.fori_loop(0, n_pages // unroll, group, init)

        m_el = pages(lambda i, acc: jnp.maximum(acc, sc_s[i]), jnp.full((R, PH), -jnp.inf, F32))
        m = jnp.maximum(jnp.max(m_el, axis=-1, keepdims=True), s_new[:, 0:1])
        l_el = pages(lambda i, acc: acc + jnp.exp(sc_s[i] - m), jnp.zeros((R, PH), F32))
        e_new = jnp.exp(s_new - m)
        l = jnp.sum(l_el, axis=-1, keepdims=True) + e_new[:, 0:1]
        inv = 1.0 / l
        lam = _lambda(lp_ref[...], lam_init)

        def weights(i, carry):
            pr = jnp.exp(sc_s[i] - m) * inv
            a = pr[:H] - lam * pr[H:]
            a_s[i] = jnp.concatenate([a, jnp.zeros_like(a)], axis=0).astype(BF16)
            return carry

        pages(weights, 0)
        pn = e_new * inv
        new_s[...] = jnp.concatenate([pn[:H] - lam * pn[H:], jnp.zeros((R - H, LANES), F32)], axis=0)

    @pl.when(p >= n_steps)
    def _():
        acc = acc_s[...]
        for g in range(G):
            vp = vc_refs[g][...].reshape(PH, 2 * dh).astype(BF16)
            acc = acc + _dot(a_s[(p - n_steps) * G + g], vp)
        acc_s[...] = acc

    @pl.when(p == 2 * n_steps - 1)
    def _():
        a_new = new_s[...][:H, 0:1].astype(BF16).astype(F32)
        o = acc_s[...][:H] + a_new * vn_ref[...].astype(BF16).astype(F32)
        ms = jnp.mean(o * o, axis=-1, keepdims=True)
        o_ref[...] = (o * lax.rsqrt(ms + RMS_EPS) * sg_ref[...] * (1.0 - lam_init)).astype(o_ref.dtype)


def _diff_attn_sample(page_table, q_sh, k_sh, v_new, cos, sin, q_gain, k_gain, sub_gain, lam_p,
                      cache_k, cache_v, *, layer, lam_init):
    dh = DA_HEAD_DIM
    B, n_pages = page_table.shape
    G = max(g for g in range(1, PAGES_PER_STEP + 1) if n_pages % g == 0)
    H = v_new.shape[1]
    R = 2 * H
    page = cache_k.shape[2]
    n_steps = n_pages // G
    per_b = lambda shape: pl.BlockSpec((None,) + shape, lambda b, p, pt: (b, 0, 0))
    full = lambda shape: pl.BlockSpec(shape, lambda b, p, pt: (0, 0))

    def k_page(g):
        return pl.BlockSpec((None, None, page, H, 2 * dh),
                            lambda b, p, pt: (layer, pt[b, jnp.minimum(p, n_steps - 1) * G + g], 0, 0, 0))

    def v_page(g):
        return pl.BlockSpec((None, None, page, H, 2 * dh),
                            lambda b, p, pt: (layer, pt[b, jnp.maximum(p - n_steps, 0) * G + g], 0, 0, 0))

    grid_spec = pltpu.PrefetchScalarGridSpec(
        num_scalar_prefetch=1,
        grid=(B, 2 * n_steps),
        in_specs=[per_b((R, dh)), per_b((R, dh)), per_b((H, 2 * dh)),
                  full((1, dh)), full((1, dh)), full((1, dh)), full((1, dh)),
                  full((1, 2 * dh)), full((4, dh))]
                 + [k_page(g) for g in range(G)] + [v_page(g) for g in range(G)],
        out_specs=[per_b((H, 2 * dh)), per_b((R, dh))],
        scratch_shapes=[pltpu.VMEM((R, 2 * dh), BF16),
                        pltpu.VMEM((n_pages, R, page * H), F32),
                        pltpu.VMEM((n_pages, R, page * H), BF16),
                        pltpu.VMEM((R, 2 * dh), F32),
                        pltpu.VMEM((R, LANES), F32)])
    o, k_out = pl.pallas_call(
        functools.partial(_decode_kernel, n_steps=n_steps, pages_per_step=G, n_heads=H, lam_init=lam_init),
        grid_spec=grid_spec,
        out_shape=[jax.ShapeDtypeStruct((B, H, 2 * dh), BF16),
                   jax.ShapeDtypeStruct((B, R, dh), F32)],
        compiler_params=_params(("arbitrary", "arbitrary")),
        name="diff_attn_sample",
    )(page_table, q_sh, k_sh, v_new, cos, sin, q_gain.reshape(1, dh), k_gain.reshape(1, dh),
      sub_gain.reshape(1, 2 * dh), lam_p, *([cache_k] * G), *([cache_v] * G))
    return o, k_out


def _rope_tables(pos):
    half = DA_HEAD_DIM // 2
    inv_freq = ROPE_THETA ** (-jnp.arange(half, dtype=F32) / half)
    ang = pos.astype(F32)[:, None] * inv_freq[None, :]
    cos = jnp.cos(ang)
    sin = jnp.sin(ang)
    return jnp.concatenate([cos, cos], axis=1), jnp.concatenate([-sin, sin], axis=1)


def _pad_tail(t):
    return jnp.concatenate([t, jnp.zeros((TAIL - t.shape[0],) + t.shape[1:], t.dtype)], axis=0)


def kernel(x_prompt, x_sample, cache_k, cache_v, state_delta, state_conv, page_table, attn_norm, w_in, dn_conv, dn_a_log, dn_dt_bias, dn_out_norm, da_q_norm, da_k_norm, da_lambda, da_sub_norm, w_branch_a, w_branch_b, w_out, ffn_norm, w_gate_up, w_down):
    B, L, D = x_prompt.shape
    DB = x_sample.shape[0]
    depth = w_in.shape[0]
    assert x_sample.shape[1] == 1 and DB <= TAIL and L % DN_CHUNK == 0
    n_pages = page_table.shape[1]
    page = cache_k.shape[2]
    past_len = n_pages * page
    dn_w = dn_conv.shape[2] // 3
    dn_h = dn_w // DN_HEAD_DIM
    da_h = cache_k.shape[3]
    da_w = da_h * 2 * DA_HEAD_DIM
    d_ff = w_down.shape[1]
    off_beta = 4 * dn_w
    shift = 2 * dn_h
    n3 = 3 * da_w + 2 * D
    assert shift < LANES and w_in.shape[2] == off_beta + shift + n3

    m = B * L
    tm = _pick_tile(m, 1024, unit=TAIL)
    x = x_prompt.reshape(m, D)
    xs = _pad_tail(x_sample.reshape(DB, D))
    cos_p, sin_p = _rope_tables(jnp.arange(L))
    cos_s, sin_s = _rope_tables(past_len + jnp.arange(1))
    tq = _pick_tile(L, 512)
    tk = _pick_tile(tq, 256)
    zero_buf = jnp.zeros((B, CONV_PAD, 3 * dn_w), F32)
    zero_state = jnp.zeros((B, dn_h, DN_HEAD_DIM, DN_HEAD_DIM), F32)
    seq_s = DN_CHUNK
    tn3 = _pick_tile(math.gcd(n3, off_beta), 512)
    w_down_bf = w_down.astype(BF16)
    w_in_t = jnp.swapaxes(w_in, 1, 2)

    ksm, vsm, sp, ssm, cp, csm = [], [], [], [], [], []
    kv_rows = None
    for l in range(depth):
        lam_init = 0.8 - 0.6 * math.exp(-0.3 * l)
        h, hs = _rmsnorm(x, xs, attn_norm[l])
        proj1, proj1_s = _matmul(h, hs, w_in_t, l, n_cols=off_beta, tn=_pick_tile(off_beta, 512), tm=tm,
                                 out_dtype=F32, transposed=True, name="mm_proj_dn")
        ba, ba_s = _matmul(h, hs, w_in_t, l, n_cols=LANES, tn=LANES, tm=tm, out_dtype=F32,
                           b_col0=off_beta, transposed=True, name="mm_proj_ba")
        proj3, proj3_s = _matmul(h, hs, w_in_t, l, n_cols=n3, tn=tn3, tm=tm, out_dtype=F32,
                                 b_col0=off_beta, shift=shift, transposed=True, name="mm_proj_da")

        o_dn, s_p = _delta_branch(proj1, ba, zero_buf, dn_conv[l], zero_state, dn_a_log[l], dn_dt_bias[l],
                                  dn_out_norm[l], batch=B, seq=L, valid=L, n_heads=dn_h)
        buf_s = jnp.concatenate([jnp.zeros((DB, CONV_PAD - (CONV_WIDTH - 1), 3 * dn_w), F32),
                                 state_conv[l]], axis=1)
        o_dn_sq, s_s = _delta_branch(proj1_s, ba_s, buf_s, dn_conv[l], state_delta[l],
                                     dn_a_log[l], dn_dt_bias[l], dn_out_norm[l],
                                     batch=DB, seq=seq_s, valid=1, n_heads=dn_h, single=True)
        o_dn_s = _pad_tail(o_dn_sq.reshape(DB, seq_s, dn_w)[:, 0])
        sp.append(s_p)
        ssm.append(s_s)
        cp.append(proj1.reshape(B, L, off_beta)[:, L - (CONV_WIDTH - 1):, :3 * dn_w])
        csm.append(jnp.concatenate([state_conv[l][:, 1:], proj1_s[:DB, None, :3 * dn_w]], axis=1))

        o_da, *kv_rows = _diff_attn_prompt(proj3, cos_p, sin_p, da_q_norm[l], da_k_norm[l],
                                           da_sub_norm[l], da_lambda[l], kv_rows, layer=l, depth=depth,
                                           batch=B, seq=L, n_heads=da_h, lam_init=lam_init, tq=tq, tk=tk)
        tail3 = proj3_s[:DB]
        to_sh = lambda t: t.reshape(DB, da_h, 2, DA_HEAD_DIM).transpose(0, 2, 1, 3).reshape(
            DB, 2 * da_h, DA_HEAD_DIM)
        v_new = tail3[:, 2 * da_w:3 * da_w].reshape(DB, da_h, 2 * DA_HEAD_DIM)
        o_da_sq, k_new = _diff_attn_sample(page_table, to_sh(tail3[:, :da_w]), to_sh(tail3[:, da_w:2 * da_w]),
                                           v_new, cos_s, sin_s, da_q_norm[l], da_k_norm[l], da_sub_norm[l],
                                           da_lambda[l], cache_k, cache_v, layer=l, lam_init=lam_init)
        o_da_s = _pad_tail(o_da_sq.reshape(DB, da_w))
        ksm.append(k_new.reshape(DB, 2, da_h, DA_HEAD_DIM).transpose(0, 2, 1, 3).reshape(
            DB, 1, da_h, 2 * DA_HEAD_DIM))
        vsm.append(v_new.reshape(DB, 1, da_h, 2 * DA_HEAD_DIM))

        tnd = _pick_tile(D, 512)
        merged, merged_s = _matmul_merge(o_dn, o_da, o_dn_s, o_da_s, w_branch_a, w_branch_b, l,
                                         proj3, proj3_s, gate_col0=3 * da_w, tn=tnd, tm=tm)
        x, xs = _matmul(merged, merged_s, w_out, l, n_cols=D, tn=tnd, tm=tm, out_dtype=F32,
                        resid=(x, xs), name="mm_out")
        hn, hns = _rmsnorm(x, xs, ffn_norm[l])
        act, act_s = _matmul_swiglu(hn, hns, w_gate_up, l, d_ff=d_ff, tn=_pick_tile(d_ff, 256), tm=tm)
        x, xs = _matmul(act, act_s, w_down_bf, l, n_cols=D, tn=tnd,
                        tm=_pick_tile(m, 512, unit=TAIL), out_dtype=F32, resid=(x, xs), name="mm_down")

    y_prompt = x.reshape(B, L, D)
    y_sample = xs[:DB].reshape(DB, 1, D)
    k_prompt, v_prompt = (t.reshape(depth, B, L, da_h, 2 * DA_HEAD_DIM) for t in kv_rows)
    return (y_prompt, y_sample, k_prompt, v_prompt, jnp.stack(ksm), jnp.stack(vsm),
            jnp.stack(sp), jnp.stack(ssm), jnp.stack(cp), jnp.stack(csm))
```

```python
import functools
import math

import jax
import jax.numpy as jnp
from jax import lax
from jax.experimental import pallas as pl
from jax.experimental.pallas import tpu as pltpu

F32 = jnp.float32
BF16 = jnp.bfloat16

RMS_EPS = 1e-6
ROPE_THETA = 10000.0
CONV_WIDTH = 4
DN_HEAD_DIM = 128
DN_CHUNK = 64
DA_HEAD_DIM = 128
LANES = 128
SUBLANES = 8
TAIL = 16
CONV_PAD = 8
VMEM_LIMIT = 56 * 1024 * 1024
DN_HEADS_PER_STEP = 2
DN_HEADS_PER_STEP_SINGLE = 8
PAGES_PER_STEP = 8


def _pick_tile(n, cap, unit=LANES):
    best = None
    t = unit
    while t <= min(n, cap):
        if n % t == 0:
            best = t
        t += unit
    assert best is not None, (n, cap)
    return best


def _nt_dot(a, b):
    return lax.dot_general(a, b, (((1,), (1,)), ((), ())), preferred_element_type=F32)


def _dot(a, b):
    return jnp.dot(a, b, preferred_element_type=F32)


def _silu(x):
    return x * jax.nn.sigmoid(x)


def _split3(x):
    hi = x.astype(BF16)
    r = x - hi.astype(F32)
    mid = r.astype(BF16)
    return hi, mid, (r - mid.astype(F32)).astype(BF16)


def _params(sem):
    return pltpu.CompilerParams(dimension_semantics=sem, vmem_limit_bytes=VMEM_LIMIT)


def _rmsnorm_kernel(x_ref, xs_ref, g_ref, o_ref, os_ref):
    def norm(x):
        ms = jnp.mean(x * x, axis=-1, keepdims=True)
        return (x * lax.rsqrt(ms + RMS_EPS) * g_ref[...]).astype(BF16)

    o_ref[...] = norm(x_ref[...])

    @pl.when(pl.program_id(0) == 0)
    def _():
        os_ref[...] = norm(xs_ref[...])


def _rmsnorm(x, xs, gain, tr=256):
    m, d = x.shape
    return pl.pallas_call(
        _rmsnorm_kernel,
        grid=(m // tr,),
        in_specs=[pl.BlockSpec((tr, d), lambda i: (i, 0)),
                  pl.BlockSpec((TAIL, d), lambda i: (0, 0)),
                  pl.BlockSpec((1, d), lambda i: (0, 0))],
        out_specs=[pl.BlockSpec((tr, d), lambda i: (i, 0)),
                   pl.BlockSpec((TAIL, d), lambda i: (0, 0))],
        out_shape=[jax.ShapeDtypeStruct((m, d), BF16), jax.ShapeDtypeStruct((TAIL, d), BF16)],
        compiler_params=_params(("arbitrary",)),
        name="rmsnorm",
    )(x, xs, gain.reshape(1, d))


CAST_ROWS = 256
CAST_COLS = 512


def _cast_weight(b_ref, bscr, b_next=None, shift=0):
    k, tn = bscr.shape
    step = CAST_ROWS if k % CAST_ROWS == 0 else k

    def body(c, carry):
        r0 = pl.multiple_of(c * step, step)
        w = b_ref[pl.ds(r0, step), :]
        if b_next is not None:
            w = jnp.concatenate([w, b_next[pl.ds(r0, step), :]], axis=1)[:, shift:shift + tn]
        bscr[pl.ds(r0, step), :] = w.astype(BF16)
        return carry

    lax.fori_loop(0, k // step, body, 0)


def _cast_weight_t(bt_ref, bscr, bt_next=None, shift=0):
    tn, k = bscr.shape
    step = CAST_COLS if k % CAST_COLS == 0 else k
    for c0 in range(0, k, step):
        cols = slice(c0, c0 + step)
        bscr[0:tn - shift, cols] = bt_ref[shift:tn, cols].astype(BF16)
        if shift:
            bscr[tn - shift:tn, cols] = bt_next[:, cols].astype(BF16)


def _mm_plain_kernel(*refs, shift, cast, resid, transposed):
    refs = list(refs)
    a_ref, as_ref, b_ref = refs[:3]
    del refs[:3]
    bn_ref = refs.pop(0) if shift else None
    r_ref, rs_ref = (refs.pop(0), refs.pop(0)) if resid else (None, None)
    o_ref, os_ref = refs[:2]
    w = refs[2] if cast else b_ref
    first = pl.program_id(1) == 0

    if cast:
        @pl.when(first)
        def _():
            (_cast_weight_t if transposed else _cast_weight)(b_ref, w, bn_ref, shift)

    def out(a, r):
        y = _nt_dot(a[...], w[...]) if transposed else _dot(a[...], w[...])
        return y if r is None else r[...] + y

    o_ref[...] = out(a_ref, r_ref).astype(o_ref.dtype)

    @pl.when(first)
    def _():
        os_ref[...] = out(as_ref, rs_ref).astype(os_ref.dtype)


def _matmul(a, a_s, b, layer, *, n_cols, tn, tm, out_dtype, b_col0=0, shift=0, resid=None, transposed=False,
            name="mm"):
    m, k = a.shape
    assert m % tm == 0 and b_col0 % tn == 0 and n_cols % tn == 0 and 0 <= shift < LANES
    joff = b_col0 // tn
    cast = b.dtype != BF16
    assert cast or not shift
    in_specs = [pl.BlockSpec((tm, k), lambda j, i: (i, 0)),
                pl.BlockSpec((TAIL, k), lambda j, i: (0, 0))]
    args = [a, a_s, b]
    if transposed:
        assert cast and shift % SUBLANES == 0 and (not shift or (tn % shift == 0 and b_col0 % shift == 0))
        in_specs.append(pl.BlockSpec((None, tn, k), lambda j, i: (layer, j + joff, 0)))
        if shift:
            per = tn // shift
            in_specs.append(pl.BlockSpec((None, shift, k), lambda j, i: (layer, (j + joff + 1) * per, 0)))
            args.append(b)
    else:
        in_specs.append(pl.BlockSpec((None, k, tn), lambda j, i: (layer, 0, j + joff)))
        if shift:
            per = tn // LANES
            in_specs.append(pl.BlockSpec((None, k, LANES), lambda j, i: (layer, 0, (j + joff + 1) * per)))
            args.append(b)
    if resid is not None:
        in_specs += [pl.BlockSpec((tm, tn), lambda j, i: (i, j)),
                     pl.BlockSpec((TAIL, tn), lambda j, i: (0, j))]
        args += list(resid)
    return pl.pallas_call(
        functools.partial(_mm_plain_kernel, shift=shift, cast=cast, resid=resid is not None,
                          transposed=transposed),
        grid=(n_cols // tn, m // tm),
        in_specs=in_specs,
        out_specs=[pl.BlockSpec((tm, tn), lambda j, i: (i, j)),
                   pl.BlockSpec((TAIL, tn), lambda j, i: (0, j))],
        out_shape=[jax.ShapeDtypeStruct((m, n_cols), out_dtype),
                   jax.ShapeDtypeStruct((TAIL, n_cols), out_dtype)],
        scratch_shapes=[pltpu.VMEM((tn, k) if transposed else (k, tn), BF16)] if cast else [],
        compiler_params=_params(("arbitrary", "arbitrary")),
        name=name,
    )(*args)


def _mm_swiglu_kernel(a_ref, as_ref, bg_ref, bu_ref, o_ref, os_ref, sg, su):
    first = pl.program_id(1) == 0

    @pl.when(first)
    def _():
        _cast_weight(bg_ref, sg)
        _cast_weight(bu_ref, su)

    def out(a):
        return (_silu(_dot(a, sg[...])) * _dot(a, su[...])).astype(BF16)

    o_ref[...] = out(a_ref[...])

    @pl.when(first)
    def _():
        os_ref[...] = out(as_ref[...])


def _matmul_swiglu(a, a_s, w_gate_up, layer, *, d_ff, tn, tm):
    m, k = a.shape
    uoff = d_ff // tn
    return pl.pallas_call(
        _mm_swiglu_kernel,
        grid=(d_ff // tn, m // tm),
        in_specs=[pl.BlockSpec((tm, k), lambda j, i: (i, 0)),
                  pl.BlockSpec((TAIL, k), lambda j, i: (0, 0)),
                  pl.BlockSpec((None, k, tn), lambda j, i: (layer, 0, j)),
                  pl.BlockSpec((None, k, tn), lambda j, i: (layer, 0, j + uoff))],
        out_specs=[pl.BlockSpec((tm, tn), lambda j, i: (i, j)),
                   pl.BlockSpec((TAIL, tn), lambda j, i: (0, j))],
        out_shape=[jax.ShapeDtypeStruct((m, d_ff), BF16), jax.ShapeDtypeStruct((TAIL, d_ff), BF16)],
        scratch_shapes=[pltpu.VMEM((k, tn), BF16), pltpu.VMEM((k, tn), BF16)],
        compiler_params=_params(("arbitrary", "arbitrary")),
        name="mm_swiglu",
    )(a, a_s, w_gate_up, w_gate_up)


def _mm_merge_kernel(a1_ref, a2_ref, a1s_ref, a2s_ref, b1_ref, b2_ref, g1_ref, g2_ref, g1s_ref, g2s_ref,
                     o_ref, os_ref, s1, s2):
    first = pl.program_id(1) == 0

    @pl.when(first)
    def _():
        _cast_weight(b1_ref, s1)
        _cast_weight(b2_ref, s2)

    def out(a1, a2, g1, g2):
        return (jax.nn.sigmoid(g1[...]) * _dot(a1[...], s1[...])
                + jax.nn.sigmoid(g2[...]) * _dot(a2[...], s2[...])).astype(BF16)

    o_ref[...] = out(a1_ref, a2_ref, g1_ref, g2_ref)

    @pl.when(first)
    def _():
        os_ref[...] = out(a1s_ref, a2s_ref, g1s_ref, g2s_ref)


def _matmul_merge(a1, a2, a1s, a2s, b1, b2, layer, gates, gates_s, *, gate_col0, tn, tm):
    m, k = a1.shape
    n = b1.shape[2]
    assert gate_col0 % tn == 0
    g1off = gate_col0 // tn
    g2off = g1off + n // tn
    row = lambda blk: pl.BlockSpec((tm, blk), lambda j, i: (i, 0))
    tail = lambda blk: pl.BlockSpec((TAIL, blk), lambda j, i: (0, 0))
    wt = pl.BlockSpec((None, k, tn), lambda j, i: (layer, 0, j))
    return pl.pallas_call(
        _mm_merge_kernel,
        grid=(n // tn, m // tm),
        in_specs=[row(k), row(k), tail(k), tail(k), wt, wt,
                  pl.BlockSpec((tm, tn), lambda j, i: (i, j + g1off)),
                  pl.BlockSpec((tm, tn), lambda j, i: (i, j + g2off)),
                  pl.BlockSpec((TAIL, tn), lambda j, i: (0, j + g1off)),
                  pl.BlockSpec((TAIL, tn), lambda j, i: (0, j + g2off))],
        out_specs=[pl.BlockSpec((tm, tn), lambda j, i: (i, j)),
                   pl.BlockSpec((TAIL, tn), lambda j, i: (0, j))],
        out_shape=[jax.ShapeDtypeStruct((m, n), BF16), jax.ShapeDtypeStruct((TAIL, n), BF16)],
        scratch_shapes=[pltpu.VMEM((k, tn), BF16), pltpu.VMEM((k, tn), BF16)],
        compiler_params=_params(("arbitrary", "arbitrary")),
        name="mm_merge",
    )(a1, a2, a1s, a2s, b1, b2, gates, gates, gates_s, gates_s)


DN_SUPER = 2
DN_INTERLEAVE = 4


def _delta_kernel(hpar_ref,
                  qp_ref, kp_ref, vp_ref, z_ref, ba_ref,
                  cbq_ref, cbk_ref, cbv_ref, cwq_ref, cwk_ref, cwv_ref,
                  s0_ref, gn_ref,
                  o_ref, sn_ref,
                  xq_s, xk_s, xv_s, aq_s, n_s, oc_s, gt_s, mf_s, mb_s,
                  *, seq, valid, n_heads, heads_per_step, single):
    C = DN_CHUNK
    D = DN_HEAD_DIM
    P = heads_per_step
    hp = pl.program_id(1)
    n_chunks = seq // C
    U = min(DN_SUPER, n_chunks)
    R = U * C
    AQ = D + C

    def rows(ref, r0, n):
        if not single:
            return ref[pl.ds(r0, n), :]
        assert n == seq
        first = lax.broadcasted_iota(jnp.int32, (seq, 1), 0) == 0
        return jnp.where(first, ref[pl.ds(pl.program_id(0), 1), :], 0.0)

    for xs, cb, xp in ((xq_s, cbq_ref, qp_ref), (xk_s, cbk_ref, kp_ref), (xv_s, cbv_ref, vp_ref)):
        xs[0:CONV_PAD, :] = cb[...]
        xs[CONV_PAD:, :] = rows(xp, 0, seq)

    row = lax.broadcasted_iota(jnp.int32, (R, R), 0)
    col = lax.broadcasted_iota(jnp.int32, (R, R), 1)
    same = (row // C) == (col // C)
    mf_s[0] = (same & (row >= col)).astype(F32)
    mf_s[1] = (same & (row > col)).astype(F32)
    mf_s[2] = (row == col).astype(F32)
    mb_s[0] = (same & (row >= col)).astype(BF16)
    mb_s[1] = (same & (row <= col)).astype(BF16)
    mb_s[2] = same.astype(BF16)
    ones_bf = jnp.ones((R, R), BF16)
    lane = lax.broadcasted_iota(jnp.int32, (R, LANES), 1)

    def conv_silu(xs, cw_ref, r0):
        xc = xs[pl.ds(r0, R + CONV_PAD), :]
        w = cw_ref[...]
        first = CONV_PAD - (CONV_WIDTH - 1)
        y = xc[first:first + R] * w[0:1]
        for t in range(1, CONV_WIDTH):
            y = y + xc[first + t:first + t + R] * w[t:t + 1]
        return _silu(y)

    def l2n(x):
        return x * lax.rsqrt(jnp.sum(x * x, axis=-1, keepdims=True) + RMS_EPS)

    def exact_dot(sel_bf, terms):
        return sum(_dot(sel_bf, t) for t in terms)

    def each(fn, *lists):
        return [fn(*args) for args in zip(*lists)]

    S = DN_INTERLEAVE if (n_chunks // U) % DN_INTERLEAVE == 0 else 1
    units = [(s, p) for s in range(S) for p in range(P)]

    def prep_stages(it):
        r0s = [pl.multiple_of((it * S + s) * R, R) for s in range(S)]
        qc = [conv_silu(xq_s, cwq_ref, r0) for r0 in r0s]
        kc = [conv_silu(xk_s, cwk_ref, r0) for r0 in r0s]
        vc = [conv_silu(xv_s, cwv_ref, r0) for r0 in r0s]
        ba = [rows(ba_ref, r0, R) for r0 in r0s]
        yield
        q = [l2n(qc[s][:, p * D:(p + 1) * D]) * (D ** -0.5) for s, p in units]
        k = [l2n(kc[s][:, p * D:(p + 1) * D]) for s, p in units]
        v = [vc[s][:, p * D:(p + 1) * D] for s, p in units]
        neg_a = -jnp.exp(hpar_ref[0:1, :])
        beta_all = [jax.nn.sigmoid(t) for t in ba]
        xs_all = [t + hpar_ref[1:2, :] for t in ba]
        g_all = [neg_a * (jnp.maximum(x, 0.0) + jnp.log1p(jnp.exp(-jnp.abs(x)))) for x in xs_all]
        beta, g = [], []
        for s, p in units:
            h = hp * P + p
            beta.append(jnp.sum(jnp.where(lane == h, beta_all[s], 0.0), axis=-1, keepdims=True))
            g.append(jnp.sum(jnp.where(lane == n_heads + h, g_all[s], 0.0), axis=-1, keepdims=True))
        if valid < seq:
            live = [(r0s[s] + lax.broadcasted_iota(jnp.int32, (R, 1), 0)) < valid for s, _ in units]
            dead = lambda t, keep: jnp.where(keep, t, 0.0)
            k, v, beta, g = each(dead, k, live), each(dead, v, live), each(dead, beta, live), each(dead, g, live)
        yield

        widen = lambda t: jnp.concatenate([t] * (R // D), axis=1) if R > D else t[:, :R]
        g3 = each(lambda t: _split3(jnp.broadcast_to(t, (R, D))), g)
        gsum = each(lambda t3: exact_dot(mb_s[0], t3), g3)
        g_last = each(lambda t3: exact_dot(mb_s[2], t3), g3)
        gj = each(lambda t3: exact_dot(ones_bf, [widen(t) * mb_s[1] for t in t3]), g3)
        yield
        decay = each(lambda a, b: jnp.exp(jnp.where(mf_s[0] > 0.5, widen(a) - b, -jnp.inf)), gsum, gj)
        kbf = each(lambda t: t.astype(BF16), k)
        m = each(lambda kb, d, b: _nt_dot(kb, kb) * d * b * mf_s[1], kbf, decay, beta)
        yield
        x_pow = each(lambda t: -t, m)
        t_inv = each(lambda t: mf_s[2] + t, x_pow)
        for _ in range(int(math.log2(C)) - 1):
            xb = each(lambda t: t.astype(BF16), x_pow)
            x_pow = each(lambda t: _dot(t, t), xb)
            t_inv = each(lambda t, xp: t + _dot(t.astype(BF16), xp.astype(BF16)), t_inv, x_pow)
            yield
        m_hi = each(lambda t: t.astype(BF16), m)
        m_lo = each(lambda t, hi: (t - hi.astype(F32)).astype(BF16), m, m_hi)
        t_hi = each(lambda t: t.astype(BF16), t_inv)
        t_lo = each(lambda t, hi: (t - hi.astype(F32)).astype(BF16), t_inv, t_hi)
        resid = each(lambda t, mh, ml, th, tl: mf_s[2] - t - (_dot(mh, th) + (_dot(mh, tl) + _dot(ml, th))),
                     t_inv, m_hi, m_lo, t_hi, t_lo)
        yield
        t_inv = each(lambda t, th, r: t + _dot(th, r.astype(BF16)), t_inv, t_hi, resid)

        yield
        e_g = each(jnp.exp, gsum)
        wu = each(lambda t, kk, vv, b, e: _dot(t.astype(BF16), jnp.concatenate(
            [kk * (b * e), vv * b], axis=1).astype(BF16)).astype(BF16), t_inv, k, v, beta, e_g)
        yield
        qk = each(lambda qq, kb, d: (_nt_dot(qq.astype(BF16), kb) * d).astype(BF16), q, kbf, decay)
        qwo = each(_dot, qk, wu)
        k_dec = each(lambda kk, gl, gs: (kk * jnp.exp(gl - gs)).astype(BF16), k, g_last, gsum)
        g_tot = each(jnp.exp, g_last)
        q_eff = each(lambda qq, e, t: (qq * e - t[:, :D]).astype(BF16), q, e_g, qwo)
        yield
        for i, (s, p) in enumerate(units):
            oc_s[p, pl.ds(r0s[s], R), :] = qwo[i][:, D:]
        for u in range(U):
            rws = slice(u * C, (u + 1) * C)
            an = each(lambda kd, w: lax.dot_general(kd[rws], w[rws], (((0,), (0,)), ((), ())),
                                                    preferred_element_type=F32), k_dec, wu)
            for i, (s, p) in enumerate(units):
                c = (it * S + s) * U + u
                base = pl.multiple_of(c * (P * AQ) + p * AQ, SUBLANES)
                aq_s[pl.ds(base, D), :] = an[i][:, :D].astype(BF16)
                aq_s[pl.ds(base + D, C), :] = q_eff[i][rws]
                n_s[p, pl.ds(pl.multiple_of(c * D, D), D), :] = an[i][:, D:]
                gt_s[p, pl.ds(pl.multiple_of(c * SUBLANES, SUBLANES), SUBLANES), :] = (
                    g_tot[i][u * C:u * C + SUBLANES])
            yield

    gain = gn_ref[...]

    def scan(c, s):
        r0 = pl.multiple_of(c * C, C)
        z = rows(z_ref, r0, C)
        r = _dot(aq_s[pl.ds(pl.multiple_of(c * (P * AQ), SUBLANES), P * AQ), :], s.astype(BF16))
        new_s = []
        outs = []
        for p in range(P):
            rp = r[p * AQ:(p + 1) * AQ, p * D:(p + 1) * D]
            g_tot = gt_s[p, pl.ds(pl.multiple_of(c * SUBLANES, SUBLANES), 1), :]
            new_s.append(s[:, p * D:(p + 1) * D] * g_tot - rp[:D]
                         + n_s[p, pl.ds(pl.multiple_of(c * D, D), D), :])
            o = rp[D:] + oc_s[p, pl.ds(r0, C), :]
            ms = jnp.mean(o * o, axis=-1, keepdims=True)
            y = o * lax.rsqrt(ms + RMS_EPS) * gain
            outs.append(y * _silu(z[:, p * D:(p + 1) * D]))
        o_ref[pl.ds(r0, C), :] = jnp.concatenate(outs, axis=1).astype(o_ref.dtype)
        return jnp.concatenate(new_s, axis=1)

    group = U * S
    n_groups = n_chunks // group

    def prep_with_scan(it, s):
        c0 = (it - 1) * group
        done = 0
        for _ in prep_stages(it):
            if done < group:
                s = scan(c0 + done, s)
                done += 1
        for c in range(done, group):
            s = scan(c0 + c, s)
        return s

    for _ in prep_stages(jnp.int32(0)):
        pass
    state = jnp.concatenate([s0_ref[p] for p in range(P)], axis=1)
    state = lax.fori_loop(1, n_groups, prep_with_scan, state)
    final = lax.fori_loop((n_groups - 1) * group, n_chunks, scan, state)
    for p in range(P):
        sn_ref[p] = final[:, p * D:(p + 1) * D]


def _delta_branch(proj1, ba, conv_buf8, conv_w, s0, a_log, dt_bias, out_gain, *, batch, seq, valid, n_heads,
                  single=False):
    D = DN_HEAD_DIM
    C = DN_CHUNK
    H = n_heads
    P = min(DN_HEADS_PER_STEP_SINGLE, H) if single else DN_HEADS_PER_STEP
    G = H // P
    PD = P * D
    assert H % P == 0
    n_chunks = seq // C
    assert n_chunks % min(DN_SUPER, n_chunks) == 0
    rr = min(DN_SUPER, n_chunks) * C
    if single:
        assert valid == 1 and seq == C
        blk = lambda off: pl.BlockSpec((TAIL, PD), lambda b, h: (0, off + h))
        ba_spec = pl.BlockSpec((TAIL, LANES), lambda b, h: (0, 0))
    else:
        blk = lambda off: pl.BlockSpec((seq, PD), lambda b, h: (b, off + h))
        ba_spec = pl.BlockSpec((seq, LANES), lambda b, h: (b, 0))
    cbs = lambda off: pl.BlockSpec((None, CONV_PAD, PD), lambda b, h: (b, 0, off + h))
    cws = lambda off: pl.BlockSpec((CONV_WIDTH, PD), lambda b, h: (0, off + h))
    hpar = jnp.zeros((SUBLANES, LANES), F32).at[0, H:2 * H].set(a_log).at[1, H:2 * H].set(dt_bias)
    o, s_new = pl.pallas_call(
        functools.partial(_delta_kernel, seq=seq, valid=valid, n_heads=H, heads_per_step=P, single=single),
        grid=(batch, G),
        in_specs=[pl.BlockSpec((SUBLANES, LANES), lambda b, h: (0, 0)),
                  blk(0), blk(G), blk(2 * G), blk(3 * G), ba_spec,
                  cbs(0), cbs(G), cbs(2 * G), cws(0), cws(G), cws(2 * G),
                  pl.BlockSpec((None, P, D, D), lambda b, h: (b, h, 0, 0)),
                  pl.BlockSpec((1, D), lambda b, h: (0, 0))],
        out_specs=[pl.BlockSpec((seq, PD), lambda b, h: (b, h)),
                   pl.BlockSpec((None, P, D, D), lambda b, h: (b, h, 0, 0))],
        out_shape=[jax.ShapeDtypeStruct((batch * seq, H * D), BF16),
                   jax.ShapeDtypeStruct((batch, H, D, D), F32)],
        scratch_shapes=[pltpu.VMEM((seq + CONV_PAD, PD), F32)] * 3 + [
            pltpu.VMEM((n_chunks * P * (D + C), D), BF16),
            pltpu.VMEM((P, n_chunks * D, D), F32),
            pltpu.VMEM((P, seq, D), F32),
            pltpu.VMEM((P, n_chunks * SUBLANES, LANES), F32),
            pltpu.VMEM((3, rr, rr), F32), pltpu.VMEM((3, rr, rr), BF16)],
        compiler_params=_params(("arbitrary", "arbitrary")),
        name="delta",
    )(hpar, proj1, proj1, proj1, proj1, ba,
      conv_buf8, conv_buf8, conv_buf8, conv_w, conv_w, conv_w, s0, out_gain.reshape(1, D))
    return o, s_new


def _norm_rope(x, gain, cos, sin):
    ms = jnp.mean(x * x, axis=-1, keepdims=True)
    y = x * lax.rsqrt(ms + RMS_EPS) * gain
    return y * cos + pltpu.roll(y, DA_HEAD_DIM // 2, 1) * sin


def _lambda(lp, lam_init):
    return (jnp.exp(jnp.sum(lp[0:1] * lp[1:2], axis=1, keepdims=True))
            - jnp.exp(jnp.sum(lp[2:3] * lp[3:4], axis=1, keepdims=True)) + lam_init)


def _da_kernel(q_ref, k_ref, v_ref, cos_ref, sin_ref, qg_ref, kg_ref, sg_ref, lp_ref, *refs,
               seq, tq, tk, lam_init):
    o_ref, krow_ref, vrow_ref, kbf_s, vbf_s = refs[-5:]
    dh = DA_HEAD_DIM
    qi = pl.program_id(2)
    per = tq // tk
    scale = dh ** -0.5

    @pl.when(qi == 0)
    def _():
        kg = kg_ref[...]

        def body(c, carry):
            r0 = pl.multiple_of(c * tk, tk)
            kb = k_ref[pl.ds(r0, tk), :]
            cos = cos_ref[pl.ds(r0, tk), :]
            sin = sin_ref[pl.ds(r0, tk), :]
            kr = jnp.concatenate([_norm_rope(kb[:, :dh], kg, cos, sin),
                                  _norm_rope(kb[:, dh:], kg, cos, sin)], axis=1)
            krow_ref[pl.ds(r0, tk), :] = kr
            kbf_s[pl.ds(r0, tk), :] = kr.astype(BF16)
            vb = v_ref[pl.ds(r0, tk), :]
            vrow_ref[pl.ds(r0, tk), :] = vb
            vbf_s[pl.ds(r0, tk), :] = vb.astype(BF16)
            return carry

        lax.fori_loop(0, seq // tk, body, 0)

    q0 = pl.multiple_of(qi * tq, tq)
    qb = q_ref[...]
    cos = cos_ref[pl.ds(q0, tq), :]
    sin = sin_ref[pl.ds(q0, tq), :]
    qg = qg_ref[...]
    q1 = _norm_rope(qb[:, :dh], qg, cos, sin).astype(BF16)
    q2 = _norm_rope(qb[:, dh:], qg, cos, sin).astype(BF16)
    lam = _lambda(lp_ref[...], lam_init)
    q_pos = q0 + lax.broadcasted_iota(jnp.int32, (tq, tk), 0)
    k_off = lax.broadcasted_iota(jnp.int32, (tq, tk), 1)

    def scores(kt, masked):
        kb = kbf_s[pl.ds(pl.multiple_of(kt * tk, tk), tk), :]
        s1 = _nt_dot(q1, kb[:, :dh]) * scale
        s2 = _nt_dot(q2, kb[:, dh:]) * scale
        if masked:
            visible = kt * tk + k_off <= q_pos
            s1 = jnp.where(visible, s1, -jnp.inf)
            s2 = jnp.where(visible, s2, -jnp.inf)
        return s1, s2

    def fold(x, op):
        y = x[:, :LANES]
        for c in range(1, tk // LANES):
            y = op(y, x[:, c * LANES:(c + 1) * LANES])
        return y

    def over_tiles(step, carry):
        def group(j, c):
            for u in range(per):
                c = step(j * per + u, c, False)
            return c

        carry = lax.fori_loop(0, qi, group, carry)
        for u in range(per):
            carry = step(qi * per + u, carry, True)
        return carry

    def max_step(kt, carry, masked):
        s1, s2 = scores(kt, masked)
        return jnp.maximum(carry[0], fold(s1, jnp.maximum)), jnp.maximum(carry[1], fold(s2, jnp.maximum))

    neg = jnp.full((tq, LANES), -jnp.inf, F32)
    mx = over_tiles(max_step, (neg, neg))
    m1 = jnp.max(mx[0], axis=-1, keepdims=True)
    m2 = jnp.max(mx[1], axis=-1, keepdims=True)

    def sum_step(kt, carry, masked):
        l1, l2, a1, a2 = carry
        s1, s2 = scores(kt, masked)
        e1 = jnp.exp(s1 - m1)
        e2 = jnp.exp(s2 - m2)
        vb = vbf_s[pl.ds(pl.multiple_of(kt * tk, tk), tk), :]
        return (l1 + fold(e1, jnp.add), l2 + fold(e2, jnp.add),
                a1 + _dot(e1.astype(BF16), vb), a2 + _dot(e2.astype(BF16), vb))

    zl = jnp.zeros((tq, LANES), F32)
    za = jnp.zeros((tq, 2 * dh), F32)
    l1, l2, a1, a2 = over_tiles(sum_step, (zl, zl, za, za))
    inv1 = 1.0 / jnp.sum(l1, axis=-1, keepdims=True)
    inv2 = 1.0 / jnp.sum(l2, axis=-1, keepdims=True)
    o = a1 * inv1 - lam * (a2 * inv2)
    ms = jnp.mean(o * o, axis=-1, keepdims=True)
    o_ref[...] = (o * lax.rsqrt(ms + RMS_EPS) * sg_ref[...] * (1.0 - lam_init)).astype(o_ref.dtype)


def _diff_attn_prompt(proj3, cos, sin, q_gain, k_gain, sub_gain, lam_p, kv_rows, *, layer, depth, batch, seq,
                      n_heads, lam_init, tq, tk):
    dh = DA_HEAD_DIM
    H = n_heads
    nq = seq // tq
    full = lambda shape: pl.BlockSpec(shape, lambda b, h, qi: (0, 0))
    n_in = 9
    carried = [] if kv_rows is None else list(kv_rows)
    rows_spec = pl.BlockSpec((None, seq, 2 * dh), lambda b, h, qi: (layer, b, h))
    rows_shape = jax.ShapeDtypeStruct((depth, batch * seq, H * 2 * dh), F32)
    return pl.pallas_call(
        functools.partial(_da_kernel, seq=seq, tq=tq, tk=tk, lam_init=lam_init),
        grid=(batch, H, nq),
        in_specs=[pl.BlockSpec((tq, 2 * dh), lambda b, h, qi: (b * nq + qi, h)),
                  pl.BlockSpec((seq, 2 * dh), lambda b, h, qi: (b, H + h)),
                  pl.BlockSpec((seq, 2 * dh), lambda b, h, qi: (b, 2 * H + h)),
                  full((seq, dh)), full((seq, dh)), full((1, dh)), full((1, dh)),
                  full((1, 2 * dh)), full((4, dh))] + [pl.BlockSpec(memory_space=pl.ANY)] * len(carried),
        out_specs=[pl.BlockSpec((tq, 2 * dh), lambda b, h, qi: (b * nq + qi, h)), rows_spec, rows_spec],
        out_shape=[jax.ShapeDtypeStruct((batch * seq, H * 2 * dh), BF16), rows_shape, rows_shape],
        input_output_aliases={n_in + i: 1 + i for i in range(len(carried))},
        scratch_shapes=[pltpu.VMEM((seq, 2 * dh), BF16), pltpu.VMEM((seq, 2 * dh), BF16)],
        compiler_params=_params(("arbitrary", "arbitrary", "arbitrary")),
        name="diff_attn_prompt",
    )(proj3, proj3, proj3, cos, sin, q_gain.reshape(1, dh), k_gain.reshape(1, dh),
      sub_gain.reshape(1, 2 * dh), lam_p, *carried)


def _decode_kernel(pt_ref, q_ref, kn_ref, vn_ref, cos_ref, sin_ref, qg_ref, kg_ref, sg_ref, lp_ref,
                   *refs, n_steps, pages_per_step, n_heads, lam_init):
    G = pages_per_step
    kc_refs, vc_refs = refs[:G], refs[G:2 * G]
    o_ref, kout_ref, qmat_s, sc_s, a_s, acc_s, new_s = refs[2 * G:]
    dh = DA_HEAD_DIM
    H = n_heads
    R = 2 * H
    page = kc_refs[0].shape[0]
    PH = page * H
    p = pl.program_id(1)
    scale = dh ** -0.5
    own_head = (lax.broadcasted_iota(jnp.int32, (R, PH), 1) % H
                == lax.broadcasted_iota(jnp.int32, (R, PH), 0) % H)

    @pl.when(p == 0)
    def _():
        cos = cos_ref[...]
        sin = sin_ref[...]
        qn = _norm_rope(q_ref[...], qg_ref[...], cos, sin)
        kn = _norm_rope(kn_ref[...], kg_ref[...], cos, sin)
        kout_ref[...] = kn
        sub = lax.broadcasted_iota(jnp.int32, (R, 2 * dh), 0) // H
        half = lax.broadcasted_iota(jnp.int32, (R, 2 * dh), 1) // dh
        qmat_s[...] = jnp.where(sub == half, jnp.concatenate([qn, qn], axis=1), 0.0).astype(BF16)
        s_new = jnp.sum(qn.astype(BF16).astype(F32) * kn.astype(BF16).astype(F32),
                        axis=-1, keepdims=True) * scale
        new_s[...] = jnp.broadcast_to(s_new, (R, LANES))
        acc_s[...] = jnp.zeros_like(acc_s)

    @pl.when(p < n_steps)
    def _():
        for g in range(G):
            kp = kc_refs[g][...].reshape(PH, 2 * dh).astype(BF16)
            s = _nt_dot(qmat_s[...], kp) * scale
            sc_s[p * G + g] = jnp.where(own_head, s, -jnp.inf)

    @pl.when(p == n_steps - 1)
    def _():
        n_pages = n_steps * G
        s_new = new_s[...]
        unroll = 8 if n_pages % 8 == 0 else 1

        def pages(fn, init):
            def group(j, acc):
                for u in range(unroll):
                    acc = fn(j * unroll + u, acc)
                return acc
            return lax.fori_loop(0, n_pages // unroll, group, init)

        m_el = pages(lambda i, acc: jnp.maximum(acc, sc_s[i]), jnp.full((R, PH), -jnp.inf, F32))
        m = jnp.maximum(jnp.max(m_el, axis=-1, keepdims=True), s_new[:, 0:1])
        l_el = pages(lambda i, acc: acc + jnp.exp(sc_s[i] - m), jnp.zeros((R, PH), F32))
        e_new = jnp.exp(s_new - m)
        l = jnp.sum(l_el, axis=-1, keepdims=True) + e_new[:, 0:1]
        inv = 1.0 / l
        lam = _lambda(lp_ref[...], lam_init)

        def weights(i, carry):
            pr = jnp.exp(sc_s[i] - m) * inv
            a = pr[:H] - lam * pr[H:]
            a_s[i] = jnp.concatenate([a, jnp.zeros_like(a)], axis=0).astype(BF16)
            return carry

        pages(weights, 0)
        pn = e_new * inv
        new_s[...] = jnp.concatenate([pn[:H] - lam * pn[H:], jnp.zeros((R - H, LANES), F32)], axis=0)

    @pl.when(p >= n_steps)
    def _():
        acc = acc_s[...]
        for g in range(G):
            vp = vc_refs[g][...].reshape(PH, 2 * dh).astype(BF16)
            acc = acc + _dot(a_s[(p - n_steps) * G + g], vp)
        acc_s[...] = acc

    @pl.when(p == 2 * n_steps - 1)
    def _():
        a_new = new_s[...][:H, 0:1].astype(BF16).astype(F32)
        o = acc_s[...][:H] + a_new * vn_ref[...].astype(BF16).astype(F32)
        ms = jnp.mean(o * o, axis=-1, keepdims=True)
        o_ref[...] = (o * lax.rsqrt(ms + RMS_EPS) * sg_ref[...] * (1.0 - lam_init)).astype(o_ref.dtype)


def _diff_attn_sample(page_table, q_sh, k_sh, v_new, cos, sin, q_gain, k_gain, sub_gain, lam_p,
                      cache_k, cache_v, *, layer, lam_init):
    dh = DA_HEAD_DIM
    B, n_pages = page_table.shape
    G = max(g for g in range(1, PAGES_PER_STEP + 1) if n_pages % g == 0)
    H = v_new.shape[1]
    R = 2 * H
    page = cache_k.shape[2]
    n_steps = n_pages // G
    per_b = lambda shape: pl.BlockSpec((None,) + shape, lambda b, p, pt: (b, 0, 0))
    full = lambda shape: pl.BlockSpec(shape, lambda b, p, pt: (0, 0))

    def k_page(g):
        return pl.BlockSpec((None, None, page, H, 2 * dh),
                            lambda b, p, pt: (layer, pt[b, jnp.minimum(p, n_steps - 1) * G + g], 0, 0, 0))

    def v_page(g):
        return pl.BlockSpec((None, None, page, H, 2 * dh),
                            lambda b, p, pt: (layer, pt[b, jnp.maximum(p - n_steps, 0) * G + g], 0, 0, 0))

    grid_spec = pltpu.PrefetchScalarGridSpec(
        num_scalar_prefetch=1,
        grid=(B, 2 * n_steps),
        in_specs=[per_b((R, dh)), per_b((R, dh)), per_b((H, 2 * dh)),
                  full((1, dh)), full((1, dh)), full((1, dh)), full((1, dh)),
                  full((1, 2 * dh)), full((4, dh))]
                 + [k_page(g) for g in range(G)] + [v_page(g) for g in range(G)],
        out_specs=[per_b((H, 2 * dh)), per_b((R, dh))],
        scratch_shapes=[pltpu.VMEM((R, 2 * dh), BF16),
                        pltpu.VMEM((n_pages, R, page * H), F32),
                        pltpu.VMEM((n_pages, R, page * H), BF16),
                        pltpu.VMEM((R, 2 * dh), F32),
                        pltpu.VMEM((R, LANES), F32)])
    o, k_out = pl.pallas_call(
        functools.partial(_decode_kernel, n_steps=n_steps, pages_per_step=G, n_heads=H, lam_init=lam_init),
        grid_spec=grid_spec,
        out_shape=[jax.ShapeDtypeStruct((B, H, 2 * dh), BF16),
                   jax.ShapeDtypeStruct((B, R, dh), F32)],
        compiler_params=_params(("arbitrary", "arbitrary")),
        name="diff_attn_sample",
    )(page_table, q_sh, k_sh, v_new, cos, sin, q_gain.reshape(1, dh), k_gain.reshape(1, dh),
      sub_gain.reshape(1, 2 * dh), lam_p, *([cache_k] * G), *([cache_v] * G))
    return o, k_out


def _rope_tables(pos):
    half = DA_HEAD_DIM // 2
    inv_freq = ROPE_THETA ** (-jnp.arange(half, dtype=F32) / half)
    ang = pos.astype(F32)[:, None] * inv_freq[None, :]
    cos = jnp.cos(ang)
    sin = jnp.sin(ang)
    return jnp.concatenate([cos, cos], axis=1), jnp.concatenate([-sin, sin], axis=1)


def _pad_tail(t):
    return jnp.concatenate([t, jnp.zeros((TAIL - t.shape[0],) + t.shape[1:], t.dtype)], axis=0)


def kernel(x_prompt, x_sample, cache_k, cache_v, state_delta, state_conv, page_table, attn_norm, w_in, dn_conv, dn_a_log, dn_dt_bias, dn_out_norm, da_q_norm, da_k_norm, da_lambda, da_sub_norm, w_branch_a, w_branch_b, w_out, ffn_norm, w_gate_up, w_down):
    B, L, D = x_prompt.shape
    DB = x_sample.shape[0]
    depth = w_in.shape[0]
    assert x_sample.shape[1] == 1 and DB <= TAIL and L % DN_CHUNK == 0
    n_pages = page_table.shape[1]
    page = cache_k.shape[2]
    past_len = n_pages * page
    dn_w = dn_conv.shape[2] // 3
    dn_h = dn_w // DN_HEAD_DIM
    da_h = cache_k.shape[3]
    da_w = da_h * 2 * DA_HEAD_DIM
    d_ff = w_down.shape[1]
    off_beta = 4 * dn_w
    shift = 2 * dn_h
    n3 = 3 * da_w + 2 * D
    assert shift < LANES and w_in.shape[2] == off_beta + shift + n3

    m = B * L
    tm = _pick_tile(m, 1024, unit=TAIL)
    x = x_prompt.reshape(m, D)
    xs = _pad_tail(x_sample.reshape(DB, D))
    cos_p, sin_p = _rope_tables(jnp.arange(L))
    cos_s, sin_s = _rope_tables(past_len + jnp.arange(1))
    tq = _pick_tile(L, 512)
    tk = _pick_tile(tq, 256)
    zero_buf = jnp.zeros((B, CONV_PAD, 3 * dn_w), F32)
    zero_state = jnp.zeros((B, dn_h, DN_HEAD_DIM, DN_HEAD_DIM), F32)
    seq_s = DN_CHUNK
    tn3 = _pick_tile(math.gcd(n3, off_beta), 512)
    w_down_bf = w_down.astype(BF16)
    w_in_t = jnp.swapaxes(w_in, 1, 2)

    ksm, vsm, sp, ssm, cp, csm = [], [], [], [], [], []
    kv_rows = None
    for l in range(depth):
        lam_init = 0.8 - 0.6 * math.exp(-0.3 * l)
        h, hs = _rmsnorm(x, xs, attn_norm[l])
        proj1, proj1_s = _matmul(h, hs, w_in_t, l, n_cols=off_beta, tn=_pick_tile(off_beta, 512), tm=tm,
                                 out_dtype=F32, transposed=True, name="mm_proj_dn")
        ba, ba_s = _matmul(h, hs, w_in_t, l, n_cols=LANES, tn=LANES, tm=tm, out_dtype=F32,
                           b_col0=off_beta, transposed=True, name="mm_proj_ba")
        proj3, proj3_s = _matmul(h, hs, w_in_t, l, n_cols=n3, tn=tn3, tm=tm, out_dtype=F32,
                                 b_col0=off_beta, shift=shift, transposed=True, name="mm_proj_da")

        o_dn, s_p = _delta_branch(proj1, ba, zero_buf, dn_conv[l], zero_state, dn_a_log[l], dn_dt_bias[l],
                                  dn_out_norm[l], batch=B, seq=L, valid=L, n_heads=dn_h)
        buf_s = jnp.concatenate([jnp.zeros((DB, CONV_PAD - (CONV_WIDTH - 1), 3 * dn_w), F32),
                                 state_conv[l]], axis=1)
        o_dn_sq, s_s = _delta_branch(proj1_s, ba_s, buf_s, dn_conv[l], state_delta[l],
                                     dn_a_log[l], dn_dt_bias[l], dn_out_norm[l],
                                     batch=DB, seq=seq_s, valid=1, n_heads=dn_h, single=True)
        o_dn_s = _pad_tail(o_dn_sq.reshape(DB, seq_s, dn_w)[:, 0])
        sp.append(s_p)
        ssm.append(s_s)
        cp.append(proj1.reshape(B, L, off_beta)[:, L - (CONV_WIDTH - 1):, :3 * dn_w])
        csm.append(jnp.concatenate([state_conv[l][:, 1:], proj1_s[:DB, None, :3 * dn_w]], axis=1))

        o_da, *kv_rows = _diff_attn_prompt(proj3, cos_p, sin_p, da_q_norm[l], da_k_norm[l],
                                           da_sub_norm[l], da_lambda[l], kv_rows, layer=l, depth=depth,
                                           batch=B, seq=L, n_heads=da_h, lam_init=lam_init, tq=tq, tk=tk)
        tail3 = proj3_s[:DB]
        to_sh = lambda t: t.reshape(DB, da_h, 2, DA_HEAD_DIM).transpose(0, 2, 1, 3).reshape(
            DB, 2 * da_h, DA_HEAD_DIM)
        v_new = tail3[:, 2 * da_w:3 * da_w].reshape(DB, da_h, 2 * DA_HEAD_DIM)
        o_da_sq, k_new = _diff_attn_sample(page_table, to_sh(tail3[:, :da_w]), to_sh(tail3[:, da_w:2 * da_w]),
                                           v_new, cos_s, sin_s, da_q_norm[l], da_k_norm[l], da_sub_norm[l],
                                           da_lambda[l], cache_k, cache_v, layer=l, lam_init=lam_init)
        o_da_s = _pad_tail(o_da_sq.reshape(DB, da_w))
        ksm.append(k_new.reshape(DB, 2, da_h, DA_HEAD_DIM).transpose(0, 2, 1, 3).reshape(
            DB, 1, da_h, 2 * DA_HEAD_DIM))
        vsm.append(v_new.reshape(DB, 1, da_h, 2 * DA_HEAD_DIM))

        tnd = _pick_tile(D, 512)
        merged, merged_s = _matmul_merge(o_dn, o_da, o_dn_s, o_da_s, w_branch_a, w_branch_b, l,
                                         proj3, proj3_s, gate_col0=3 * da_w, tn=tnd, tm=tm)
        x, xs = _matmul(merged, merged_s, w_out, l, n_cols=D, tn=tnd, tm=tm, out_dtype=F32,
                        resid=(x, xs), name="mm_out")
        hn, hns = _rmsnorm(x, xs, ffn_norm[l])
        act, act_s = _matmul_swiglu(hn, hns, w_gate_up, l, d_ff=d_ff, tn=_pick_tile(d_ff, 256), tm=tm)
        x, xs = _matmul(act, act_s, w_down_bf, l, n_cols=D, tn=tnd,
                        tm=_pick_tile(m, 512, unit=TAIL), out_dtype=F32, resid=(x, xs), name="mm_down")

    y_prompt = x.reshape(B, L, D)
    y_sample = xs[:DB].reshape(DB, 1, D)
    k_prompt, v_prompt = (t.reshape(depth, B, L, da_h, 2 * DA_HEAD_DIM) for t in kv_rows)
    return (y_prompt, y_sample, k_prompt, v_prompt, jnp.stack(ksm), jnp.stack(vsm),
            jnp.stack(sp), jnp.stack(ssm), jnp.stack(cp), jnp.stack(csm))
```
